```python
import math, functools
import jax, jax.numpy as jnp
from jax import lax
import numpy as np

D_MODEL = 1024
BATCH = 2
SEQ = 8192
DEPTH = 4
DEC_BATCH = 32
DEC_SEQ = 8
PAST_LEN = 8192
PAGE_SIZE = 128

HG_HEADS = 4
HG_DK = 128
HG_DV = 128
HG_WK = HG_HEADS * HG_DK
HG_WV = HG_HEADS * HG_DV
HG_CHUNK = 64
DA_HEADS = 4
DA_DQK = 64
DA_DV = 2 * DA_DQK
DA_WQK = DA_HEADS * 2 * DA_DQK
DA_WV = DA_HEADS * DA_DV
ROT_DIM = DA_DQK // 4
ROPE_THETA = 500000.0
Q_BLOCK = 128
EPS = 1e-6
IN_SPLITS = (HG_WK, HG_WK, HG_WV, HG_WV, DA_WQK, DA_WQK, DA_WV, DA_WV, D_MODEL, D_MODEL)
D_IN = sum(IN_SPLITS)

kernel_name = "hgrn2_diffattn_gated_hybrid_step"


def rmsnorm(x, g):
    xf = x.astype(jnp.float32)
    y = xf * lax.rsqrt(jnp.mean(xf * xf, axis=-1, keepdims=True) + EPS) * g.astype(jnp.float32)
    return y.astype(x.dtype)


def rope_partial(x, pos):
    inv_freq = ROPE_THETA ** (-jnp.arange(0, ROT_DIM, 2, dtype=jnp.float32) / ROT_DIM)
    ang = pos.astype(jnp.float32)[:, None] * inv_freq[None, :]
    cos = jnp.cos(ang)[:, None, :]
    sin = jnp.sin(ang)[:, None, :]
    xr = x[..., :ROT_DIM].astype(jnp.float32)
    x1, x2 = xr[..., :ROT_DIM // 2], xr[..., ROT_DIM // 2:]
    rot = jnp.concatenate([x1 * cos - x2 * sin, x2 * cos + x1 * sin], axis=-1).astype(x.dtype)
    return jnp.concatenate([rot, x[..., ROT_DIM:]], axis=-1)


def hgrn2_chunked(q, logf, k, v, s0, chunk):
    B, L, H, DK = q.shape
    n = L // chunk

    def to_chunks(a):
        return a.astype(jnp.float32).reshape(B, n, chunk, H, a.shape[-1]).transpose(1, 0, 3, 2, 4)

    tri = jnp.tril(jnp.ones((chunk, chunk), dtype=bool))[:, :, None]

    def step(S, inp):
        qc, gc, kc, vc = inp
        bcum = jnp.cumsum(gc, axis=2)
        rel = bcum[:, :, :, None, :] - bcum[:, :, None, :, :]
        decay = jnp.exp(jnp.where(tri, rel, -jnp.inf))
        A = jnp.einsum('bhtk,bhsk,bhtsk->bhts', qc, kc, decay)
        o = jnp.einsum('bhts,bhsv->bhtv', A, vc) + jnp.einsum('bhtk,bhkv->bhtv', qc * jnp.exp(bcum), S)
        btot = bcum[:, :, -1]
        S_new = jnp.exp(btot)[..., None] * S + jnp.einsum(
            'bhsk,bhsv->bhkv', kc * jnp.exp(btot[:, :, None, :] - bcum), vc)
        return S_new, o

    S_fin, o = lax.scan(step, s0, (to_chunks(q), to_chunks(logf), to_chunks(k), to_chunks(v)))
    o = o.transpose(1, 0, 3, 2, 4).reshape(B, L, H, v.shape[-1])
    return o, S_fin


def diff_attn_prompt(q, k, v, lam):
    B, S = q.shape[0], q.shape[1]
    nb = S // Q_BLOCK
    scale = 1.0 / math.sqrt(DA_DQK)
    qb = q.reshape(B, nb, Q_BLOCK, DA_HEADS, 2, DA_DQK).transpose(1, 0, 2, 3, 4, 5)
    kpos = jnp.arange(S)

    def block(args):
        qblk, bi = args
        s = jnp.einsum('bqhmd,bkhmd->bhmqk', qblk, k).astype(jnp.float32) * scale
        qpos = bi * Q_BLOCK + jnp.arange(Q_BLOCK)
        mask = kpos[None, :] <= qpos[:, None]
        p = jax.nn.softmax(jnp.where(mask, s, -jnp.inf), axis=-1)
        a = p[:, :, 0] - lam * p[:, :, 1]
        return jnp.einsum('bhqk,bkhd->bqhd', a.astype(v.dtype), v)

    out = lax.map(block, (qb, jnp.arange(nb)))
    return out.transpose(1, 0, 2, 3, 4).reshape(B, S, DA_HEADS, DA_DV)


def diff_attn_sample(q, k, v, lam, k_past, v_past):
    T = q.shape[1]
    P = k_past.shape[1]
    scale = 1.0 / math.sqrt(DA_DQK)
    s_past = jnp.einsum('bqhmd,bkhmd->bhmqk', q, k_past).astype(jnp.float32)
    s_new = jnp.einsum('bqhmd,bkhmd->bhmqk', q, k).astype(jnp.float32)
    causal = jnp.tril(jnp.ones((T, T), dtype=bool))
    s_new = jnp.where(causal, s_new, -jnp.inf)
    p = jax.nn.softmax(jnp.concatenate([s_past, s_new], axis=-1) * scale, axis=-1)
    a = (p[:, :, 0] - lam * p[:, :, 1]).astype(v.dtype)
    return (jnp.einsum('bhqk,bkhd->bqhd', a[..., :P], v_past)
            + jnp.einsum('bhqk,bkhd->bqhd', a[..., P:], v))


def trunk_layer(x, l, pos, s0, hg_chunk, attend, lb, norm_g, w_in, hg_norm_g,
                lq1, lk1, lq2, lk2, subln_g, w_a, w_b, w_out):
    B, L, _ = x.shape
    h = rmsnorm(x, norm_g)
    proj = jnp.einsum('bld,de->ble', h, w_in)
    split_idx = [int(i) for i in np.cumsum(IN_SPLITS)[:-1]]
    hq, hf, hi, hz, dq, dk, dv, dz, ga, gb = jnp.split(proj, split_idx, axis=-1)

    f = lb + (1.0 - lb) * jax.nn.sigmoid(hf.astype(jnp.float32))
    logf = jnp.log(f).reshape(B, L, HG_HEADS, HG_DK)
    kh = (1.0 - f).reshape(B, L, HG_HEADS, HG_DK)
    o_h, S_fin = hgrn2_chunked(hq.reshape(B, L, HG_HEADS, HG_DK), logf, kh,
                               hi.reshape(B, L, HG_HEADS, HG_DV), s0, hg_chunk)
    o_h = rmsnorm(o_h.astype(x.dtype), hg_norm_g).reshape(B, L, HG_WV) * jax.nn.silu(hz)

    q = rope_partial(dq.reshape(B, L, DA_HEADS * 2, DA_DQK), pos).reshape(B, L, DA_HEADS, 2, DA_DQK)
    k = rope_partial(dk.reshape(B, L, DA_HEADS * 2, DA_DQK), pos).reshape(B, L, DA_HEADS, 2, DA_DQK)
    v = dv.reshape(B, L, DA_HEADS, DA_DV)
    lam_init = 0.8 - 0.6 * math.exp(-0.3 * l)
    lam = (jnp.exp(jnp.sum(lq1.astype(jnp.float32) * lk1.astype(jnp.float32)))
           - jnp.exp(jnp.sum(lq2.astype(jnp.float32) * lk2.astype(jnp.float32))) + lam_init)
    o_d = attend(q, k, v, lam)
    o_d = (rmsnorm(o_d, subln_g) * (1.0 - lam_init)).reshape(B, L, DA_WV) * jax.nn.silu(dz)

    merged = (jax.nn.sigmoid(ga) * jnp.einsum('ble,ed->bld', o_h, w_a)
              + jax.nn.sigmoid(gb) * jnp.einsum('ble,ed->bld', o_d, w_b))
    y = jnp.einsum('bld,de->ble', merged, w_out)
    return x + y, k.reshape(B, L, DA_HEADS, 2 * DA_DQK), v, S_fin


def setup_inputs(seed: int = 0) -> dict:
    key = jax.random.key(seed)
    ks = jax.random.split(key, 20)
    n_pages = PAST_LEN // PAGE_SIZE
    n_used = DEC_BATCH * n_pages
    n_pool = (n_used * 5) // 4
    nrm = jax.random.normal
    f32 = jnp.float32
    x_prompt = nrm(ks[0], (BATCH, SEQ, D_MODEL), f32)
    x_sample = nrm(ks[1], (DEC_BATCH, DEC_SEQ, D_MODEL), f32)
    cache_k = nrm(ks[2], (DEPTH, n_pool, PAGE_SIZE, DA_HEADS, 2 * DA_DQK), f32)
    cache_v = nrm(ks[3], (DEPTH, n_pool, PAGE_SIZE, DA_HEADS, DA_DV), f32)
    state_hgrn = 0.5 * nrm(ks[4], (DEPTH, DEC_BATCH, HG_HEADS, HG_DK, HG_DV), f32)
    page_table = jax.random.permutation(ks[5], n_pool)[:n_used].reshape(DEC_BATCH, n_pages).astype(jnp.int32)
    norm_g = 1.0 + 0.02 * nrm(ks[6], (DEPTH, D_MODEL), f32)
    w_in = nrm(ks[7], (DEPTH, D_MODEL, D_IN), f32) * D_MODEL ** -0.5
    hgrn_lb_logits = nrm(ks[8], (DEPTH, HG_WK), f32)
    hgrn_norm_g = 1.0 + 0.02 * nrm(ks[9], (DEPTH, HG_DV), f32)
    da_lambda_q1 = 0.1 * nrm(ks[10], (DEPTH, DA_DQK), f32)
    da_lambda_k1 = 0.1 * nrm(ks[11], (DEPTH, DA_DQK), f32)
    da_lambda_q2 = 0.1 * nrm(ks[12], (DEPTH, DA_DQK), f32)
    da_lambda_k2 = 0.1 * nrm(ks[13], (DEPTH, DA_DQK), f32)
    da_subln_g = 1.0 + 0.02 * nrm(ks[14], (DEPTH, DA_DV), f32)
    w_branch_hgrn = nrm(ks[15], (DEPTH, HG_WV, D_MODEL), f32) * HG_WV ** -0.5
    w_branch_attn = nrm(ks[16], (DEPTH, DA_WV, D_MODEL), f32) * DA_WV ** -0.5
    w_out = nrm(ks[17], (DEPTH, D_MODEL, D_MODEL), f32) * (0.5 * D_MODEL ** -0.5)
    final_norm_g = 1.0 + 0.02 * nrm(ks[18], (D_MODEL,), f32)
    return {"x_prompt": x_prompt, "x_sample": x_sample, "cache_k": cache_k, "cache_v": cache_v,
            "state_hgrn": state_hgrn, "page_table": page_table, "norm_g": norm_g, "w_in": w_in,
            "hgrn_lb_logits": hgrn_lb_logits, "hgrn_norm_g": hgrn_norm_g,
            "da_lambda_q1": da_lambda_q1, "da_lambda_k1": da_lambda_k1,
            "da_lambda_q2": da_lambda_q2, "da_lambda_k2": da_lambda_k2, "da_subln_g": da_subln_g,
            "w_branch_hgrn": w_branch_hgrn, "w_branch_attn": w_branch_attn, "w_out": w_out,
            "final_norm_g": final_norm_g}


def reference(x_prompt, x_sample, cache_k, cache_v, state_hgrn, page_table, norm_g, w_in,
              hgrn_lb_logits, hgrn_norm_g, da_lambda_q1, da_lambda_k1, da_lambda_q2, da_lambda_k2,
              da_subln_g, w_branch_hgrn, w_branch_attn, w_out, final_norm_g):
    Bp, Sp, _ = x_prompt.shape
    Bs, T, _ = x_sample.shape
    past = page_table.shape[1] * cache_k.shape[2]
    pos_p = jnp.arange(Sp)
    pos_s = past + jnp.arange(T)
    lb_all = jnp.cumsum(jax.nn.softmax(hgrn_lb_logits.astype(jnp.float32), axis=0), axis=0)
    lb_all = lb_all - lb_all[0:1]

    xp, xs = x_prompt, x_sample
    kp_l, vp_l, sp_l, ks_l, vs_l, ss_l = [], [], [], [], [], []
    for l in range(DEPTH):
        w = (lb_all[l], norm_g[l], w_in[l], hgrn_norm_g[l], da_lambda_q1[l], da_lambda_k1[l],
             da_lambda_q2[l], da_lambda_k2[l], da_subln_g[l], w_branch_hgrn[l], w_branch_attn[l], w_out[l])
        s0_p = jnp.zeros((Bp, HG_HEADS, HG_DK, HG_DV), jnp.float32)
        xp, kp, vp, sp = trunk_layer(xp, l, pos_p, s0_p, min(HG_CHUNK, Sp), diff_attn_prompt, *w)
        k_past = cache_k[l][page_table].reshape(Bs, past, DA_HEADS, 2, DA_DQK)
        v_past = cache_v[l][page_table].reshape(Bs, past, DA_HEADS, DA_DV)
        attend_s = functools.partial(diff_attn_sample, k_past=k_past, v_past=v_past)
        xs, kss, vss, ss = trunk_layer(xs, l, pos_s, state_hgrn[l].astype(jnp.float32), T, attend_s, *w)
        kp_l.append(kp); vp_l.append(vp); sp_l.append(sp.astype(state_hgrn.dtype))
        ks_l.append(kss); vs_l.append(vss); ss_l.append(ss.astype(state_hgrn.dtype))

    y_prompt = rmsnorm(xp, final_norm_g)
    y_sample = rmsnorm(xs, final_norm_g)
    return (y_prompt, y_sample, jnp.stack(kp_l), jnp.stack(vp_l), jnp.stack(sp_l),
            jnp.stack(ks_l), jnp.stack(vs_l), jnp.stack(ss_l))
```

```python
import functools
import math

import jax
import jax.numpy as jnp
from jax import lax
from jax.experimental import pallas as pl
from jax.experimental.pallas import tpu as pltpu

F32 = jnp.float32
BF16 = jnp.bfloat16

EPS = 1e-6
ROPE_THETA = 500000.0
HEADS = 4
HEAD_W = 128
SUB_W = 64
ROT_DIM = SUB_W // 4
BRANCH_W = HEADS * HEAD_W
NEG = -1e30

V7X_VMEM_LIMIT = 56 * 1024 * 1024
SUBLANES = 8


def _dot(a, b):
    return jnp.dot(a, b, preferred_element_type=F32)


def _dot_nt(a, b):
    return lax.dot_general(a, b, (((1,), (1,)), ((), ())), preferred_element_type=F32)


def _iota(shape, dim):
    return lax.broadcasted_iota(jnp.int32, shape, dim)


def _rms(x, g):
    return x * lax.rsqrt(jnp.mean(x * x, axis=-1, keepdims=True) + EPS) * g


def _lam(lq1_ref, lk1_ref, lq2_ref, lk2_ref, layer, lam_init):
    r = slice(layer, layer + 1)
    a = jnp.sum(lq1_ref[r, :] * lk1_ref[r, :], axis=-1, keepdims=True)
    b = jnp.sum(lq2_ref[r, :] * lk2_ref[r, :], axis=-1, keepdims=True)
    return jnp.exp(a) - jnp.exp(b) + lam_init


def _inproj_kernel(x_ref, g_ref, w_ref, lbl_ref, cos_ref, sa_ref, sb_ref,
                   hq_ref, logf_ref, kh_ref, hi_ref, hz_ref, q_ref, k_ref, kb_ref, v_ref, vb_ref,
                   dz_ref, ga_ref, gb_ref, *, layer):
    h = _rms(x_ref[...], g_ref[...]).astype(BF16)
    W = BRANCH_W

    def proj(i, n=W):
        return _dot(h, w_ref[:, i * W:i * W + n])

    lbl = lbl_ref[...]
    e = jnp.exp(lbl - jnp.max(lbl, axis=0, keepdims=True))
    sm = e / jnp.sum(e, axis=0, keepdims=True)
    lb = jnp.zeros((1, W), F32)
    for i in range(1, layer + 1):
        lb = lb + sm[i:i + 1, :]

    hq_ref[...] = proj(0)
    f = lb + (1.0 - lb) * jax.nn.sigmoid(proj(1))
    logf_ref[...] = jnp.log(f)
    kh_ref[...] = 1.0 - f
    hi_ref[...] = proj(2)
    hz_ref[...] = proj(3)

    cos, sa, sb = cos_ref[...], sa_ref[...], sb_ref[...]

    def rope(x):
        parts = []
        for hd in range(HEADS):
            xs = x[:, hd * HEAD_W:(hd + 1) * HEAD_W]
            up = pltpu.roll(xs, HEAD_W - ROT_DIM // 2, 1)
            dn = pltpu.roll(xs, ROT_DIM // 2, 1)
            parts.append(xs * cos + up * sa + dn * sb)
        return jnp.concatenate(parts, axis=1)

    q_ref[...] = rope(proj(4)).astype(BF16)
    k = rope(proj(5))
    k_ref[...] = k
    kb_ref[...] = k.astype(BF16)
    v = proj(6)
    v_ref[...] = v
    vb_ref[...] = v.astype(BF16)
    dz_ref[...] = proj(7)
    ga_ref[...] = proj(8, 2 * W)
    gb_ref[...] = proj(10, 2 * W)


def _inproj(x2d, norm_g, w_bf, lb_logits, cos_t, sa_t, sb_t, layer, tm):
    n, d = x2d.shape
    d_in = w_bf.shape[1]
    n_tab = cos_t.shape[0] // tm
    W = BRANCH_W
    row = lambda i: (i, 0)
    const = lambda i: (0, 0)
    tab = lambda i: (i % n_tab, 0)
    f32o = lambda w: jax.ShapeDtypeStruct((n, w), F32)
    bfo = lambda w: jax.ShapeDtypeStruct((n, w), BF16)
    out_shape = (f32o(W), f32o(W), f32o(W), f32o(W), f32o(W), bfo(W), f32o(W), bfo(W), f32o(W), bfo(W),
                 f32o(W), f32o(2 * W), f32o(2 * W))
    out_specs = tuple(pl.BlockSpec((tm, s.shape[1]), row) for s in out_shape)
    return pl.pallas_call(
        functools.partial(_inproj_kernel, layer=layer),
        grid=(n // tm,),
        in_specs=[pl.BlockSpec((tm, d), row), pl.BlockSpec((1, d), const), pl.BlockSpec((d, d_in), const),
                  pl.BlockSpec(lb_logits.shape, const),
                  pl.BlockSpec((tm, HEAD_W), tab), pl.BlockSpec((tm, HEAD_W), tab), pl.BlockSpec((tm, HEAD_W), tab)],
        out_specs=out_specs,
        out_shape=out_shape,
        compiler_params=pltpu.CompilerParams(dimension_semantics=("parallel",), vmem_limit_bytes=V7X_VMEM_LIMIT),
        name="inproj",
    )(x2d, norm_g, w_bf, lb_logits, cos_t, sa_t, sb_t)


def _hgrn_kernel(q_ref, g_ref, k_ref, v_ref, s0_ref, o_ref, sfin_ref, st_scr, bc_scr, *, chunk, sub, n_chunks):
    j = pl.program_id(1)

    @pl.when(j == 0)
    def _():
        for hd in range(HEADS):
            st_scr[hd] = s0_ref[0, hd].T

    C, SC = chunk, sub
    n_sub = C // SC
    tri = (_iota((C, C), 0) >= _iota((C, C), 1)).astype(F32).astype(BF16)
    col_c = _iota((SC, C), 1)
    row8 = _iota((SUBLANES, HEAD_W), 0)

    for c in range(n_chunks):
        r0 = c * C
        for hd in range(HEADS):
            hs = slice(hd * HEAD_W, (hd + 1) * HEAD_W)
            q = q_ref[r0:r0 + C, hs]
            g = g_ref[r0:r0 + C, hs]
            k = k_ref[r0:r0 + C, hs]
            v = v_ref[r0:r0 + C, hs]
            g0 = g.astype(BF16)
            r1 = g - g0.astype(F32)
            g1 = r1.astype(BF16)
            g2 = (r1 - g1.astype(F32)).astype(BF16)
            bc = _dot(tri, g0) + _dot(tri, g1) + _dot(tri, g2)
            bc_scr[hd] = bc
            st = st_scr[hd]
            vb = v.astype(BF16)

            o = _dot_nt((q * jnp.exp(bc)).astype(BF16), st.astype(BF16))

            if n_sub > 1:
                a_rows = [jnp.zeros((SC, C), F32)]
                for i in range(1, n_sub):
                    bref = bc_scr[hd, i * SC - 1:i * SC, :]
                    qi = q[i * SC:(i + 1) * SC] * jnp.exp(bc[i * SC:(i + 1) * SC] - bref)
                    kt = k * jnp.exp(jnp.minimum(bref - bc, 0.0))
                    a = _dot_nt(qi.astype(BF16), kt.astype(BF16))
                    a_rows.append(jnp.where(col_c < i * SC, a, 0.0))
                o = o + _dot(jnp.concatenate(a_rows, axis=0).astype(BF16), vb)

            groups = []
            for t0 in range(0, C, SUBLANES):
                c0 = (t0 // SC) * SC
                bg = bc[t0:t0 + SUBLANES]
                qg = q[t0:t0 + SUBLANES]
                acc = jnp.zeros((SUBLANES, HEAD_W), F32)
                for s in range(c0, t0 + SUBLANES):
                    d = bg - bc_scr[hd, s:s + 1, :]
                    if s >= t0:
                        d = jnp.where(row8 >= s - t0, d, NEG)
                    p = qg * jnp.exp(d) * k_ref[r0 + s:r0 + s + 1, hs]
                    acc = acc + jnp.sum(p, axis=-1, keepdims=True) * v_ref[r0 + s:r0 + s + 1, hs]
                groups.append(acc)
            o_ref[r0:r0 + C, hs] = o + jnp.concatenate(groups, axis=0)

            btot = bc_scr[hd, C - 1:C, :]
            kdec = (k * jnp.exp(btot - bc)).astype(BF16)
            st_scr[hd] = st * jnp.exp(btot) + _dot(v.T.astype(BF16), kdec)

    @pl.when(j == pl.num_programs(1) - 1)
    def _():
        for hd in range(HEADS):
            sfin_ref[0, hd] = st_scr[hd].T


def _hgrn(hq, logf, kh, hi, s0, batch, seq, chunk, sub, tt):
    n = batch * seq
    steps = seq // tt
    blk = pl.BlockSpec((tt, BRANCH_W), lambda b, j: (b * steps + j, 0))
    st_spec = pl.BlockSpec((1, HEADS, HEAD_W, HEAD_W), lambda b, j: (b, 0, 0, 0))
    return pl.pallas_call(
        functools.partial(_hgrn_kernel, chunk=chunk, sub=sub, n_chunks=tt // chunk),
        grid=(batch, steps),
        in_specs=[blk, blk, blk, blk, st_spec],
        out_specs=(blk, st_spec),
        out_shape=(jax.ShapeDtypeStruct((n, BRANCH_W), F32),
                   jax.ShapeDtypeStruct((batch, HEADS, HEAD_W, HEAD_W), F32)),
        scratch_shapes=[pltpu.VMEM((HEADS, HEAD_W, HEAD_W), F32), pltpu.VMEM((HEADS, chunk, HEAD_W), F32)],
        compiler_params=pltpu.CompilerParams(dimension_semantics=("parallel", "arbitrary"),
                                             vmem_limit_bytes=V7X_VMEM_LIMIT),
        name="hgrn",
    )(hq, logf, kh, hi, s0)


def _stack_maps(q):
    lane = _iota(q.shape, 1)
    zero = jnp.zeros_like(q)
    q1 = jnp.where(lane < SUB_W, q, zero)
    q2 = jnp.where(lane >= SUB_W, q, zero)
    return jnp.concatenate([q1, q2], axis=0) * (1.0 / math.sqrt(SUB_W))


def _softmax_step(s, vb, m_scr, l_scr, acc_scr):
    m_prev = m_scr[...]
    m_new = jnp.maximum(m_prev, jnp.max(s, axis=-1, keepdims=True))
    p = jnp.exp(s - m_new[:, :1])
    alpha = jnp.exp(m_prev - m_new)
    l_scr[...] = alpha * l_scr[...] + jnp.sum(p, axis=-1, keepdims=True)
    acc_scr[...] = alpha * acc_scr[...] + _dot(p.astype(BF16), vb)
    m_scr[...] = m_new


def _softmax_init(m_scr, l_scr, acc_scr):
    m_scr[...] = jnp.full(m_scr.shape, NEG, F32)
    l_scr[...] = jnp.zeros(l_scr.shape, F32)
    acc_scr[...] = jnp.zeros(acc_scr.shape, F32)


def _diff_combine(l_scr, acc_scr, lam, r):
    inv = 1.0 / l_scr[...]
    o = acc_scr[...] * inv
    return o[:r] - lam * o[r:]


def _flash_kernel(q_ref, k_ref, v_ref, lq1_ref, lk1_ref, lq2_ref, lk2_ref, o_ref, m_scr, l_scr, acc_scr,
                  *, tq, layer, lam_init):
    qi = pl.program_id(2)
    qs = _stack_maps(q_ref[...])
    _softmax_init(m_scr, l_scr, acc_scr)

    def body(j, carry):
        r0 = pl.multiple_of(j * tq, tq)
        s = _dot_nt(qs, k_ref[pl.ds(r0, tq), :])
        _softmax_step(s, v_ref[pl.ds(r0, tq), :], m_scr, l_scr, acc_scr)
        return carry

    lax.fori_loop(0, qi, body, 0)

    r0 = pl.multiple_of(qi * tq, tq)
    s = _dot_nt(qs, k_ref[pl.ds(r0, tq), :])
    row = _iota((2 * tq, tq), 0)
    row = jnp.where(row >= tq, row - tq, row)
    s = jnp.where(_iota((2 * tq, tq), 1) <= row, s, NEG)
    _softmax_step(s, v_ref[pl.ds(r0, tq), :], m_scr, l_scr, acc_scr)

    lam = _lam(lq1_ref, lk1_ref, lq2_ref, lk2_ref, layer, lam_init)
    o_ref[...] = _diff_combine(l_scr, acc_scr, lam, tq)


def _flash(qb, kb, vb, lams, batch, seq, layer, lam_init, tq):
    n = batch * seq
    nq = seq // tq
    lam_specs = [pl.BlockSpec(a.shape, lambda b, h, i: (0, 0)) for a in lams]
    return pl.pallas_call(
        functools.partial(_flash_kernel, tq=tq, layer=layer, lam_init=lam_init),
        grid=(batch, HEADS, nq),
        in_specs=[pl.BlockSpec((tq, HEAD_W), lambda b, h, i: (b * nq + i, h)),
                  pl.BlockSpec((seq, HEAD_W), lambda b, h, i: (b, h)),
                  pl.BlockSpec((seq, HEAD_W), lambda b, h, i: (b, h))] + lam_specs,
        out_specs=pl.BlockSpec((tq, HEAD_W), lambda b, h, i: (b * nq + i, h)),
        out_shape=jax.ShapeDtypeStruct((n, BRANCH_W), F32),
        scratch_shapes=[pltpu.VMEM((2 * tq, HEAD_W), F32), pltpu.VMEM((2 * tq, HEAD_W), F32),
                        pltpu.VMEM((2 * tq, HEAD_W), F32)],
        compiler_params=pltpu.CompilerParams(dimension_semantics=("parallel", "parallel", "arbitrary"),
                                             vmem_limit_bytes=V7X_VMEM_LIMIT),
        name="flash",
    )(qb, kb, vb, *lams)


def _decode_kernel(pt_ref, q_ref, kn_ref, vn_ref, lq1_ref, lk1_ref, lq2_ref, lk2_ref, *rest,
                   pages, layer, lam_init):
    del pt_ref
    k_refs, v_refs = rest[:pages], rest[pages:2 * pages]
    o_ref, m_scr, l_scr, acc_scr = rest[2 * pages:]
    g = pl.program_id(1)

    @pl.when(g == 0)
    def _():
        _softmax_init(m_scr, l_scr, acc_scr)

    qs = _stack_maps(q_ref[0])
    r2 = qs.shape[0]
    rows_k = k_refs[0].shape[0]
    same_head = (_iota((r2, rows_k), 0) % HEADS) == (_iota((r2, rows_k), 1) % HEADS)
    bias = jnp.where(same_head, 0.0, NEG)
    for p in range(pages):
        s = _dot_nt(qs, k_refs[p][...].astype(BF16)) + bias
        _softmax_step(s, v_refs[p][...].astype(BF16), m_scr, l_scr, acc_scr)

    @pl.when(g == pl.num_programs(1) - 1)
    def _():
        kn = kn_ref[0]
        rn = kn.shape[0]
        ri, ci = _iota((r2, rn), 0), _iota((r2, rn), 1)
        ri = jnp.where(ri >= rn, ri - rn, ri)
        s = jnp.where(ci // HEADS <= ri // HEADS, _dot_nt(qs, kn), NEG)
        s = jnp.where((ri % HEADS) == (ci % HEADS), s, NEG)
        _softmax_step(s, vn_ref[0], m_scr, l_scr, acc_scr)
        lam = _lam(lq1_ref, lk1_ref, lq2_ref, lk2_ref, layer, lam_init)
        o_ref[0] = _diff_combine(l_scr, acc_scr, lam, rn)


def _decode(q3, kn3, vn3, lams, ck, cv, page_table, layer, lam_init, pages):
    bs, rn, _ = q3.shape
    n_pages = page_table.shape[1]
    rows_k = ck.shape[2]
    new_spec = pl.BlockSpec((1, rn, HEAD_W), lambda b, g, pt: (b, 0, 0))
    lam_specs = [pl.BlockSpec(a.shape, lambda b, g, pt: (0, 0)) for a in lams]

    def page_spec(p):
        return pl.BlockSpec((None, None, rows_k, HEAD_W), lambda b, g, pt: (layer, pt[b, g * pages + p], 0, 0))

    page_specs = [page_spec(p) for p in range(pages)]
    grid_spec = pltpu.PrefetchScalarGridSpec(
        num_scalar_prefetch=1,
        grid=(bs, n_pages // pages),
        in_specs=[new_spec, new_spec, new_spec] + lam_specs + page_specs + page_specs,
        out_specs=pl.BlockSpec((1, rn, HEAD_W), lambda b, g, pt: (b, 0, 0)),
        scratch_shapes=[pltpu.VMEM((2 * rn, HEAD_W), F32), pltpu.VMEM((2 * rn, HEAD_W), F32),
                        pltpu.VMEM((2 * rn, HEAD_W), F32)],
    )
    return pl.pallas_call(
        functools.partial(_decode_kernel, pages=pages, layer=layer, lam_init=lam_init),
        grid_spec=grid_spec,
        out_shape=jax.ShapeDtypeStruct((bs, rn, HEAD_W), F32),
        compiler_params=pltpu.CompilerParams(dimension_semantics=("parallel", "arbitrary"),
                                             vmem_limit_bytes=V7X_VMEM_LIMIT),
        name="decode",
    )(page_table, q3, kn3, vn3, *lams, *([ck] * pages), *([cv] * pages))


def _outproj_kernel(oh_ref, hz_ref, od_ref, dz_ref, ga_ref, gb_ref, x_ref, hg_ref, sg_ref,
                    wa_ref, wb_ref, wo_ref, fg_ref, y_ref, *, lam_init, final):
    def headnorm(o, g):
        return jnp.concatenate([_rms(o[:, hd * HEAD_W:(hd + 1) * HEAD_W], g) for hd in range(HEADS)], axis=1)

    a = headnorm(oh_ref[...], hg_ref[...]) * jax.nn.silu(hz_ref[...])
    d = headnorm(od_ref[...], sg_ref[...]) * (1.0 - lam_init) * jax.nn.silu(dz_ref[...])
    merged = (jax.nn.sigmoid(ga_ref[...]) * _dot(a.astype(BF16), wa_ref[...])
              + jax.nn.sigmoid(gb_ref[...]) * _dot(d.astype(BF16), wb_ref[...]))
    y = x_ref[...] + _dot(merged.astype(BF16), wo_ref[...])
    y_ref[...] = _rms(y, fg_ref[...]) if final else y


def _outproj(oh, hz, od, dz, ga, gb, x2d, hg, sg, wa, wb, wo, fg, lam_init, final, tm):
    n, d = x2d.shape
    row = lambda i: (i, 0)
    const = lambda i: (0, 0)
    W = BRANCH_W
    return pl.pallas_call(
        functools.partial(_outproj_kernel, lam_init=lam_init, final=final),
        grid=(n // tm,),
        in_specs=[pl.BlockSpec((tm, W), row), pl.BlockSpec((tm, W), row), pl.BlockSpec((tm, W), row),
                  pl.BlockSpec((tm, W), row), pl.BlockSpec((tm, d), row), pl.BlockSpec((tm, d), row),
                  pl.BlockSpec((tm, d), row), pl.BlockSpec((1, HEAD_W), const), pl.BlockSpec((1, HEAD_W), const),
                  pl.BlockSpec((W, d), const), pl.BlockSpec((W, d), const), pl.BlockSpec((d, d), const),
                  pl.BlockSpec((1, d), const)],
        out_specs=pl.BlockSpec((tm, d), row),
        out_shape=jax.ShapeDtypeStruct((n, d), F32),
        compiler_params=pltpu.CompilerParams(dimension_semantics=("parallel",), vmem_limit_bytes=V7X_VMEM_LIMIT),
        name="outproj",
    )(oh, hz, od, dz, ga, gb, x2d, hg, sg, wa, wb, wo, fg)


def _rope_tables(pos):
    half = ROT_DIM // 2
    inv_freq = ROPE_THETA ** (-jnp.arange(0, ROT_DIM, 2, dtype=F32) / ROT_DIM)
    ang = pos.astype(F32)[:, None] * inv_freq[None, :]
    cos, sin = jnp.cos(ang), jnp.sin(ang)
    n = pos.shape[0]
    one = jnp.ones((n, SUB_W - ROT_DIM), F32)
    zero_h = jnp.zeros((n, half), F32)
    zero_r = jnp.zeros((n, SUB_W - ROT_DIM), F32)
    cos_t = jnp.concatenate([cos, cos, one], axis=1)
    sa_t = jnp.concatenate([-sin, zero_h, zero_r], axis=1)
    sb_t = jnp.concatenate([zero_h, sin, zero_r], axis=1)
    rep = lambda t: jnp.concatenate([t, t], axis=1)
    return rep(cos_t), rep(sa_t), rep(sb_t)


def kernel(x_prompt, x_sample, cache_k, cache_v, state_hgrn, page_table, norm_g, w_in, hgrn_lb_logits, hgrn_norm_g,
           da_lambda_q1, da_lambda_k1, da_lambda_q2, da_lambda_k2, da_subln_g, w_branch_hgrn, w_branch_attn,
           w_out, final_norm_g):
    bp, sp, d = x_prompt.shape
    bs, t, _ = x_sample.shape
    depth = w_in.shape[0]
    page = cache_k.shape[2]
    past = page_table.shape[1] * page
    W = BRANCH_W

    tm_p = min(256, bp * sp)
    tm_s = min(256, bs * t)
    chunk_p = min(64, sp)
    sub_p = min(16, chunk_p)
    tt_p = min(256, sp)
    tq = min(512, sp)
    t_pad = 16
    pages = min(8, page_table.shape[1])

    tabs_p = _rope_tables(jnp.arange(sp))
    tabs_s = tuple(jnp.tile(a, (tm_s // t, 1)) for a in _rope_tables(past + jnp.arange(t)))

    ck = cache_k.reshape(depth, cache_k.shape[1], page * HEADS, HEAD_W)
    cv = cache_v.reshape(depth, cache_v.shape[1], page * HEADS, HEAD_W)
    s0_p = jnp.zeros((bp, HEADS, HEAD_W, HEAD_W), F32)
    lams = (da_lambda_q1, da_lambda_k1, da_lambda_q2, da_lambda_k2)
    fg = final_norm_g.reshape(1, d)

    xp = x_prompt.reshape(bp * sp, d)
    xs = x_sample.reshape(bs * t, d)
    kp_l, vp_l, sp_l, ks_l, vs_l, ss_l = [], [], [], [], [], []
    for l in range(depth):
        lam_init = 0.8 - 0.6 * math.exp(-0.3 * l)
        final = l == depth - 1
        w_bf = w_in[l].astype(BF16)
        ng = norm_g[l].reshape(1, d)
        hg = hgrn_norm_g[l].reshape(1, HEAD_W)
        sg = da_subln_g[l].reshape(1, HEAD_W)
        wa = w_branch_hgrn[l].astype(BF16)
        wb = w_branch_attn[l].astype(BF16)
        wo = w_out[l].astype(BF16)

        (hq, logf, kh, hi, hz, qb, k, kb, v, vb, dz, ga, gb) = _inproj(
            xp, ng, w_bf, hgrn_lb_logits, *tabs_p, l, tm_p)
        oh, s_fin = _hgrn(hq, logf, kh, hi, s0_p, bp, sp, chunk_p, sub_p, tt_p)
        od = _flash(qb, kb, vb, lams, bp, sp, l, lam_init, tq)
        xp = _outproj(oh, hz, od, dz, ga, gb, xp, hg, sg, wa, wb, wo, fg, lam_init, final, tm_p)
        kp_l.append(k.reshape(bp, sp, HEADS, HEAD_W))
        vp_l.append(v.reshape(bp, sp, HEADS, HEAD_W))
        sp_l.append(s_fin)

        (hq, logf, kh, hi, hz, qb, k, kb, v, vb, dz, ga, gb) = _inproj(
            xs, ng, w_bf, hgrn_lb_logits, *tabs_s, l, tm_s)
        padt = lambda a: jnp.pad(a.reshape(bs, t, W), ((0, 0), (0, t_pad - t), (0, 0))).reshape(bs * t_pad, W)
        oh, s_fin = _hgrn(padt(hq), padt(logf), padt(kh), padt(hi), state_hgrn[l], bs, t_pad, t_pad, t_pad, t_pad)
        oh = oh.reshape(bs, t_pad, W)[:, :t].reshape(bs * t, W)
        rows = lambda a: a.reshape(bs, t * HEADS, HEAD_W)
        od = _decode(rows(qb), rows(kb), rows(vb), lams, ck, cv, page_table, l, lam_init, pages)
        xs = _outproj(oh, hz, od.reshape(bs * t, W), dz, ga, gb, xs, hg, sg, wa, wb, wo, fg, lam_init, final, tm_s)
        ks_l.append(k.reshape(bs, t, HEADS, HEAD_W))
        vs_l.append(v.reshape(bs, t, HEADS, HEAD_W))
        ss_l.append(s_fin)

    return (xp.reshape(bp, sp, d), xs.reshape(bs, t, d), jnp.stack(kp_l), jnp.stack(vp_l), jnp.stack(sp_l),
            jnp.stack(ks_l), jnp.stack(vs_l), jnp.stack(ss_l))
```

```python
import functools
import math

import jax
import jax.numpy as jnp
from jax import lax
from jax.experimental import pallas as pl
from jax.experimental.pallas import tpu as pltpu

F32 = jnp.float32
BF16 = jnp.bfloat16

EPS = 1e-6
ROPE_THETA = 500000.0
HEADS = 4
HEAD_W = 128
SUB_W = 64
ROT_DIM = SUB_W // 4
BRANCH_W = HEADS * HEAD_W
NEG = -1e30
QK_SCALE_LOG2 = math.log2(math.e) / math.sqrt(SUB_W)

V7X_VMEM_LIMIT = 56 * 1024 * 1024
SUBLANES = 8


def _dot(a, b):
    return jnp.dot(a, b, preferred_element_type=F32)


def _dot_nt(a, b):
    return lax.dot_general(a, b, (((1,), (1,)), ((), ())), preferred_element_type=F32)


def _iota(shape, dim):
    return lax.broadcasted_iota(jnp.int32, shape, dim)


def _rms(x, g):
    return x * lax.rsqrt(jnp.mean(x * x, axis=-1, keepdims=True) + EPS) * g


def _lam(lq1_ref, lk1_ref, lq2_ref, lk2_ref, layer, lam_init):
    r = slice(layer, layer + 1)
    a = jnp.sum(lq1_ref[r, :] * lk1_ref[r, :], axis=-1, keepdims=True)
    b = jnp.sum(lq2_ref[r, :] * lk2_ref[r, :], axis=-1, keepdims=True)
    return jnp.exp(a) - jnp.exp(b) + lam_init


def _inproj_kernel(x_ref, g_ref, w_ref, lbl_ref, cos_ref, sa_ref, sb_ref,
                   hq_ref, logf_ref, kh_ref, hi_ref, hz_ref, q_ref, k_ref, kb_ref, v_ref, vb_ref,
                   dz_ref, ga_ref, gb_ref, *, layer):
    h = _rms(x_ref[...], g_ref[...]).astype(BF16)
    W = BRANCH_W

    def proj(i, n=W):
        return _dot(h, w_ref[:, i * W:i * W + n])

    lbl = lbl_ref[...]
    e = jnp.exp(lbl - jnp.max(lbl, axis=0, keepdims=True))
    sm = e / jnp.sum(e, axis=0, keepdims=True)
    lb = jnp.zeros((1, W), F32)
    for i in range(1, layer + 1):
        lb = lb + sm[i:i + 1, :]

    hq_ref[...] = proj(0)
    f = lb + (1.0 - lb) * jax.nn.sigmoid(proj(1))
    logf_ref[...] = jnp.log(f)
    kh_ref[...] = 1.0 - f
    hi_ref[...] = proj(2)
    hz_ref[...] = proj(3)

    cos, sa, sb = cos_ref[...], sa_ref[...], sb_ref[...]

    def rope(x):
        parts = []
        for hd in range(HEADS):
            xs = x[:, hd * HEAD_W:(hd + 1) * HEAD_W]
            up = pltpu.roll(xs, HEAD_W - ROT_DIM // 2, 1)
            dn = pltpu.roll(xs, ROT_DIM // 2, 1)
            parts.append(xs * cos + up * sa + dn * sb)
        return jnp.concatenate(parts, axis=1)

    def store_head_rows(ref, x):
        for hd in range(HEADS):
            ref[pl.ds(hd, x.shape[0], stride=HEADS), :] = x[:, hd * HEAD_W:(hd + 1) * HEAD_W]

    q_ref[...] = (rope(proj(4)) * QK_SCALE_LOG2).astype(BF16)
    k = rope(proj(5))
    store_head_rows(k_ref, k)
    kb_ref[...] = k.astype(BF16)
    v = proj(6)
    store_head_rows(v_ref, v)
    vb = v.astype(BF16)
    ones = jnp.ones((vb.shape[0], HEAD_W), BF16)
    vb_ref[...] = jnp.concatenate(
        [piece for hd in range(HEADS) for piece in (vb[:, hd * HEAD_W:(hd + 1) * HEAD_W], ones)], axis=1)
    dz_ref[...] = proj(7)
    ga_ref[...] = proj(8, 2 * W)
    gb_ref[...] = proj(10, 2 * W)


def _inproj(x2d, norm_g, w_bf, lb_logits, cos_t, sa_t, sb_t, layer, tm):
    n, d = x2d.shape
    d_in = w_bf.shape[2]
    n_tab = cos_t.shape[0] // tm
    W = BRANCH_W
    row = lambda i: (i, 0)
    const = lambda i: (0, 0)
    lay = lambda i: (layer, 0, 0)
    tab = lambda i: (i % n_tab, 0)
    f32o = lambda w: jax.ShapeDtypeStruct((n, w), F32)
    bfo = lambda w: jax.ShapeDtypeStruct((n, w), BF16)
    head_rows = jax.ShapeDtypeStruct((n * HEADS, HEAD_W), F32)
    out_shape = (f32o(W), f32o(W), f32o(W), f32o(W), f32o(W), bfo(W), head_rows, bfo(W), head_rows, bfo(2 * W),
                 f32o(W), f32o(2 * W), f32o(2 * W))
    out_specs = tuple(pl.BlockSpec((tm * s.shape[0] // n, s.shape[1]), row) for s in out_shape)
    return pl.pallas_call(
        functools.partial(_inproj_kernel, layer=layer),
        grid=(n // tm,),
        in_specs=[pl.BlockSpec((tm, d), row), pl.BlockSpec((None, 1, d), lay), pl.BlockSpec((None, d, d_in), lay),
                  pl.BlockSpec(lb_logits.shape, const),
                  pl.BlockSpec((tm, HEAD_W), tab), pl.BlockSpec((tm, HEAD_W), tab), pl.BlockSpec((tm, HEAD_W), tab)],
        out_specs=out_specs,
        out_shape=out_shape,
        compiler_params=pltpu.CompilerParams(dimension_semantics=("parallel",), vmem_limit_bytes=V7X_VMEM_LIMIT),
        name="inproj",
    )(x2d, norm_g, w_bf, lb_logits, cos_t, sa_t, sb_t)


def _hgrn_kernel(q_ref, g_ref, k_ref, v_ref, s0_ref, o_ref, sfin_ref, st_scr, bc_scr, *, chunk, sub, n_chunks):
    j = pl.program_id(1)

    @pl.when(j == 0)
    def _():
        for hd in range(HEADS):
            st_scr[hd] = s0_ref[0, hd].T

    C, SC = chunk, sub
    n_sub = C // SC
    tri = (_iota((C, C), 0) >= _iota((C, C), 1)).astype(F32).astype(BF16)
    col_c = _iota((SC, C), 1)
    row8 = _iota((SUBLANES, HEAD_W), 0)

    for c in range(n_chunks):
        r0 = c * C
        for hd in range(HEADS):
            hs = slice(hd * HEAD_W, (hd + 1) * HEAD_W)
            q = q_ref[r0:r0 + C, hs]
            g = g_ref[r0:r0 + C, hs]
            k = k_ref[r0:r0 + C, hs]
            v = v_ref[r0:r0 + C, hs]
            g0 = g.astype(BF16)
            r1 = g - g0.astype(F32)
            g1 = r1.astype(BF16)
            g2 = (r1 - g1.astype(F32)).astype(BF16)
            bc = _dot(tri, g0) + _dot(tri, g1) + _dot(tri, g2)
            bc_scr[hd] = bc
            st = st_scr[hd]
            vb = v.astype(BF16)

            o = _dot_nt((q * jnp.exp(bc)).astype(BF16), st.astype(BF16))

            if n_sub > 1:
                a_rows = [jnp.zeros((SC, C), F32)]
                for i in range(1, n_sub):
                    bref = bc_scr[hd, i * SC - 1:i * SC, :]
                    qi = q[i * SC:(i + 1) * SC] * jnp.exp(bc[i * SC:(i + 1) * SC] - bref)
                    kt = k * jnp.exp(jnp.minimum(bref - bc, 0.0))
                    a = _dot_nt(qi.astype(BF16), kt.astype(BF16))
                    a_rows.append(jnp.where(col_c < i * SC, a, 0.0))
                o = o + _dot(jnp.concatenate(a_rows, axis=0).astype(BF16), vb)

            groups = []
            for t0 in range(0, C, SUBLANES):
                c0 = (t0 // SC) * SC
                bg = bc[t0:t0 + SUBLANES]
                qg = q[t0:t0 + SUBLANES]
                acc = jnp.zeros((SUBLANES, HEAD_W), F32)
                for s in range(c0, t0 + SUBLANES):
                    d = bg - bc_scr[hd, s:s + 1, :]
                    if s >= t0:
                        d = jnp.where(row8 >= s - t0, d, NEG)
                    p = qg * jnp.exp(d) * k_ref[r0 + s:r0 + s + 1, hs]
                    acc = acc + jnp.sum(p, axis=-1, keepdims=True) * v_ref[r0 + s:r0 + s + 1, hs]
                groups.append(acc)
            o_ref[r0:r0 + C, hs] = o + jnp.concatenate(groups, axis=0)

            btot = bc_scr[hd, C - 1:C, :]
            kdec = (k * jnp.exp(btot - bc)).astype(BF16)
            st_scr[hd] = st * jnp.exp(btot) + _dot(v.T.astype(BF16), kdec)

    @pl.when(j == pl.num_programs(1) - 1)
    def _():
        for hd in range(HEADS):
            sfin_ref[0, hd] = st_scr[hd].T


def _hgrn(hq, logf, kh, hi, s0, batch, seq, chunk, sub, tt):
    n = batch * seq
    steps = seq // tt
    blk = pl.BlockSpec((tt, BRANCH_W), lambda b, j: (b * steps + j, 0))
    st_spec = pl.BlockSpec((1, HEADS, HEAD_W, HEAD_W), lambda b, j: (b, 0, 0, 0))
    return pl.pallas_call(
        functools.partial(_hgrn_kernel, chunk=chunk, sub=sub, n_chunks=tt // chunk),
        grid=(batch, steps),
        in_specs=[blk, blk, blk, blk, st_spec],
        out_specs=(blk, st_spec),
        out_shape=(jax.ShapeDtypeStruct((n, BRANCH_W), F32),
                   jax.ShapeDtypeStruct((batch, HEADS, HEAD_W, HEAD_W), F32)),
        scratch_shapes=[pltpu.VMEM((HEADS, HEAD_W, HEAD_W), F32), pltpu.VMEM((HEADS, chunk, HEAD_W), F32)],
        compiler_params=pltpu.CompilerParams(dimension_semantics=("parallel", "arbitrary"),
                                             vmem_limit_bytes=V7X_VMEM_LIMIT),
        name="hgrn",
    )(hq, logf, kh, hi, s0)


def _stack_maps(q):
    lane = _iota(q.shape, 1)
    zero = jnp.zeros_like(q)
    return jnp.concatenate([jnp.where(lane < SUB_W, q, zero), jnp.where(lane >= SUB_W, q, zero)], axis=0)


def _flash_kernel(q_ref, k_ref, v_ref, lq1_ref, lk1_ref, lq2_ref, lk2_ref, o_ref, qs_scr, s_scr, m_scr, acc_scr,
                  *, tq, rb, layer, lam_init):
    qi = pl.program_id(2)
    tk = 2 * tq
    qs_scr[...] = _stack_maps(q_ref[...])
    m_scr[...] = jnp.full(m_scr.shape, NEG, F32)
    acc_scr[...] = jnp.zeros(acc_scr.shape, F32)
    n_rb = 2 * tq // rb

    def scores(r0):
        s_scr[...] = _dot_nt(qs_scr[...], k_ref[pl.ds(r0, tk), :])

    def process(r0, width, shift):
        vj = v_ref[pl.ds(r0, width), :]
        for b in range(n_rb):
            rows = slice(b * rb, (b + 1) * rb)
            s = s_scr[rows, :width]
            if shift is not None:
                tok = _iota((rb, width), 0) + ((b * rb) % tq + shift)
                s = jnp.where(_iota((rb, width), 1) <= tok, s, NEG)
            m_prev = m_scr[rows, :]
            m_new = jnp.maximum(m_prev, jnp.max(s, axis=-1, keepdims=True))
            p = jnp.exp2(s - m_new[:, :1]).astype(BF16)
            alpha = jnp.exp2(m_prev - m_new)
            m_scr[rows, :] = m_new
            pv = _dot(p, vj)
            acc_scr[rows, :HEAD_W] = alpha * acc_scr[rows, :HEAD_W] + pv[:, :HEAD_W]
            acc_scr[rows, HEAD_W:] = alpha * acc_scr[rows, HEAD_W:] + pv[:, HEAD_W:]

    scores(0)
    n_full = qi // 2

    def body(j, carry):
        r0 = pl.multiple_of(j * tk, tk)
        process(r0, tk, None)
        scores(r0 + tk)
        return carry

    lax.fori_loop(0, n_full, body, 0)
    r0 = pl.multiple_of(n_full * tk, tk)

    @pl.when(qi % 2 == 1)
    def _():
        process(r0, tk, tq)

    @pl.when(qi % 2 == 0)
    def _():
        process(r0, tq, 0)

    lam = _lam(lq1_ref, lk1_ref, lq2_ref, lk2_ref, layer, lam_init)
    o = acc_scr[:, :HEAD_W] / acc_scr[:, HEAD_W:]
    o_ref[...] = o[:tq] - lam * o[tq:]


def _flash(qb, kb, vbx, lams, batch, seq, layer, lam_init, tq, rb):
    n = batch * seq
    nq = seq // tq
    lam_specs = [pl.BlockSpec(a.shape, lambda b, h, i: (0, 0)) for a in lams]
    return pl.pallas_call(
        functools.partial(_flash_kernel, tq=tq, rb=rb, layer=layer, lam_init=lam_init),
        grid=(batch, HEADS, nq),
        in_specs=[pl.BlockSpec((tq, HEAD_W), lambda b, h, i: (b * nq + i, h)),
                  pl.BlockSpec((seq, HEAD_W), lambda b, h, i: (b, h)),
                  pl.BlockSpec((seq, 2 * HEAD_W), lambda b, h, i: (b, h))] + lam_specs,
        out_specs=pl.BlockSpec((tq, HEAD_W), lambda b, h, i: (b * nq + i, h)),
        out_shape=jax.ShapeDtypeStruct((n, BRANCH_W), F32),
        scratch_shapes=[pltpu.VMEM((2 * tq, HEAD_W), BF16), pltpu.VMEM((2 * tq, 2 * tq), F32),
                        pltpu.VMEM((2 * tq, HEAD_W), F32), pltpu.VMEM((2 * tq, 2 * HEAD_W), F32)],
        compiler_params=pltpu.CompilerParams(dimension_semantics=("parallel", "parallel", "arbitrary"),
                                             vmem_limit_bytes=V7X_VMEM_LIMIT),
        name="flash",
    )(qb, kb, vbx, *lams)


def _softmax_update(blocks, m_scr, l_scr, acc_scr):
    m_prev = m_scr[...]
    m_cur = jnp.max(functools.reduce(jnp.maximum, [s for s, _ in blocks]), axis=-1, keepdims=True)
    m_new = jnp.maximum(m_prev, m_cur)
    alpha = jnp.exp2(m_prev - m_new)
    p_sum, pv = None, None
    for s, vb in blocks:
        p = jnp.exp2(s - m_new[:, :1])
        d = _dot(p.astype(BF16), vb)
        p_sum = p if p_sum is None else p_sum + p
        pv = d if pv is None else pv + d
    l_scr[...] = alpha * l_scr[...] + jnp.sum(p_sum, axis=-1, keepdims=True)
    acc_scr[...] = alpha * acc_scr[...] + pv
    m_scr[...] = m_new


def _decode_kernel(pt_ref, q_ref, kn_ref, vn_ref, lq1_ref, lk1_ref, lq2_ref, lk2_ref, *rest,
                   pages, layer, lam_init):
    del pt_ref
    k_refs, v_refs = rest[:pages], rest[pages:2 * pages]
    o_ref, m_scr, l_scr, acc_scr = rest[2 * pages:]
    g = pl.program_id(1)

    @pl.when(g == 0)
    def _():
        m_scr[...] = jnp.full(m_scr.shape, NEG, F32)
        l_scr[...] = jnp.zeros(l_scr.shape, F32)
        acc_scr[...] = jnp.zeros(acc_scr.shape, F32)

    qs = _stack_maps(q_ref[0])
    r2 = qs.shape[0]
    rows_k = k_refs[0].shape[0]
    same_head = (_iota((r2, rows_k), 0) % HEADS) == (_iota((r2, rows_k), 1) % HEADS)
    bias = jnp.where(same_head, 0.0, NEG)
    _softmax_update([(_dot_nt(qs, k_refs[p][...].astype(BF16)) + bias, v_refs[p][...].astype(BF16))
                     for p in range(pages)], m_scr, l_scr, acc_scr)

    @pl.when(g == pl.num_programs(1) - 1)
    def _():
        kn = kn_ref[0]
        rn = kn.shape[0]
        ri, ci = _iota((r2, rn), 0), _iota((r2, rn), 1)
        ri = jnp.where(ri >= rn, ri - rn, ri)
        s = jnp.where(ci // HEADS <= ri // HEADS, _dot_nt(qs, kn), NEG)
        s = jnp.where((ri % HEADS) == (ci % HEADS), s, NEG)
        _softmax_update([(s, vn_ref[0])], m_scr, l_scr, acc_scr)
        lam = _lam(lq1_ref, lk1_ref, lq2_ref, lk2_ref, layer, lam_init)
        o = acc_scr[...] / l_scr[...]
        o_ref[0] = o[:rn] - lam * o[rn:]


def _decode(q3, kn3, vn3, lams, ck, cv, page_table, layer, lam_init, pages):
    bs, rn, _ = q3.shape
    n_pages = page_table.shape[1]
    rows_k = ck.shape[2]
    new_spec = pl.BlockSpec((1, rn, HEAD_W), lambda b, g, pt: (b, 0, 0))
    lam_specs = [pl.BlockSpec(a.shape, lambda b, g, pt: (0, 0)) for a in lams]

    def page_spec(p):
        return pl.BlockSpec((None, None, rows_k, HEAD_W), lambda b, g, pt: (layer, pt[b, g * pages + p], 0, 0))

    page_specs = [page_spec(p) for p in range(pages)]
    grid_spec = pltpu.PrefetchScalarGridSpec(
        num_scalar_prefetch=1,
        grid=(bs, n_pages // pages),
        in_specs=[new_spec, new_spec, new_spec] + lam_specs + page_specs + page_specs,
        out_specs=pl.BlockSpec((1, rn, HEAD_W), lambda b, g, pt: (b, 0, 0)),
        scratch_shapes=[pltpu.VMEM((2 * rn, HEAD_W), F32), pltpu.VMEM((2 * rn, HEAD_W), F32),
                        pltpu.VMEM((2 * rn, HEAD_W), F32)],
    )
    return pl.pallas_call(
        functools.partial(_decode_kernel, pages=pages, layer=layer, lam_init=lam_init),
        grid_spec=grid_spec,
        out_shape=jax.ShapeDtypeStruct((bs, rn, HEAD_W), F32),
        compiler_params=pltpu.CompilerParams(dimension_semantics=("parallel", "arbitrary"),
                                             vmem_limit_bytes=V7X_VMEM_LIMIT),
        name="decode",
    )(page_table, q3, kn3, vn3, *lams, *([ck] * pages), *([cv] * pages))


def _outproj_kernel(oh_ref, hz_ref, od_ref, dz_ref, ga_ref, gb_ref, x_ref, hg_ref, sg_ref,
                    wa_ref, wb_ref, wo_ref, fg_ref, y_ref, *, lam_init, final):
    def headnorm(o, g):
        return jnp.concatenate([_rms(o[:, hd * HEAD_W:(hd + 1) * HEAD_W], g) for hd in range(HEADS)], axis=1)

    a = headnorm(oh_ref[...], hg_ref[...]) * jax.nn.silu(hz_ref[...])
    d = headnorm(od_ref[...], sg_ref[...]) * (1.0 - lam_init) * jax.nn.silu(dz_ref[...])
    merged = (jax.nn.sigmoid(ga_ref[...]) * _dot(a.astype(BF16), wa_ref[...])
              + jax.nn.sigmoid(gb_ref[...]) * _dot(d.astype(BF16), wb_ref[...]))
    y = x_ref[...] + _dot(merged.astype(BF16), wo_ref[...])
    y_ref[...] = _rms(y, fg_ref[...]) if final else y


def _outproj(oh, hz, od, dz, ga, gb, x2d, hg, sg, wa, wb, wo, fg, layer, lam_init, final, tm):
    n, d = x2d.shape
    row = lambda i: (i, 0)
    const = lambda i: (0, 0)
    lay = lambda i: (layer, 0, 0)
    W = BRANCH_W
    return pl.pallas_call(
        functools.partial(_outproj_kernel, lam_init=lam_init, final=final),
        grid=(n // tm,),
        in_specs=[pl.BlockSpec((tm, W), row), pl.BlockSpec((tm, W), row), pl.BlockSpec((tm, W), row),
                  pl.BlockSpec((tm, W), row), pl.BlockSpec((tm, d), row), pl.BlockSpec((tm, d), row),
                  pl.BlockSpec((tm, d), row), pl.BlockSpec((None, 1, HEAD_W), lay), pl.BlockSpec((None, 1, HEAD_W), lay),
                  pl.BlockSpec((None, W, d), lay), pl.BlockSpec((None, W, d), lay), pl.BlockSpec((None, d, d), lay),
                  pl.BlockSpec((1, d), const)],
        out_specs=pl.BlockSpec((tm, d), row),
        out_shape=jax.ShapeDtypeStruct((n, d), F32),
        compiler_params=pltpu.CompilerParams(dimension_semantics=("parallel",), vmem_limit_bytes=V7X_VMEM_LIMIT),
        name="outproj",
    )(oh, hz, od, dz, ga, gb, x2d, hg, sg, wa, wb, wo, fg)


def _rope_tables(pos):
    dim = jnp.arange(HEAD_W) % SUB_W
    rot = dim < ROT_DIM
    inv_freq = jnp.where(rot, ROPE_THETA ** (-(2 * (dim % (ROT_DIM // 2))).astype(F32) / ROT_DIM), 0.0)
    ang = pos.astype(F32)[:, None] * inv_freq[None, :]
    sin = jnp.sin(ang)
    first = (dim < ROT_DIM // 2)[None, :]
    return jnp.cos(ang), jnp.where(first, -sin, 0.0), jnp.where(first, 0.0, sin)


def kernel(x_prompt, x_sample, cache_k, cache_v, state_hgrn, page_table, norm_g, w_in, hgrn_lb_logits, hgrn_norm_g,
           da_lambda_q1, da_lambda_k1, da_lambda_q2, da_lambda_k2, da_subln_g, w_branch_hgrn, w_branch_attn,
           w_out, final_norm_g):
    bp, sp, d = x_prompt.shape
    bs, t, _ = x_sample.shape
    depth = w_in.shape[0]
    page = cache_k.shape[2]
    past = page_table.shape[1] * page
    W = BRANCH_W

    tm_p = min(256, bp * sp)
    tm_s = min(256, bs * t)
    chunk_p = min(64, sp)
    sub_p = min(16, chunk_p)
    tt_p = min(256, sp)
    tq = min(512, sp // 2)
    rb = tq
    assert sp % (2 * tq) == 0
    t_pad = 16
    pages = min(16, page_table.shape[1])

    tabs_p = _rope_tables(jnp.arange(sp))
    tabs_s = tuple(jnp.tile(a, (tm_s // t, 1)) for a in _rope_tables(past + jnp.arange(t)))

    ck = cache_k.reshape(depth, cache_k.shape[1], page * HEADS, HEAD_W)
    cv = cache_v.reshape(depth, cache_v.shape[1], page * HEADS, HEAD_W)
    s0_p = jnp.zeros((bp, HEADS, HEAD_W, HEAD_W), F32)
    lams = (da_lambda_q1, da_lambda_k1, da_lambda_q2, da_lambda_k2)
    fg = final_norm_g.reshape(1, d)
    w_bf = w_in.astype(BF16)
    ng = norm_g.reshape(depth, 1, d)
    hg = hgrn_norm_g.reshape(depth, 1, HEAD_W)
    sg = da_subln_g.reshape(depth, 1, HEAD_W)
    wa = w_branch_hgrn.astype(BF16)
    wb = w_branch_attn.astype(BF16)
    wo = w_out.astype(BF16)

    xp = x_prompt.reshape(bp * sp, d)
    xs = x_sample.reshape(bs * t, d)
    kp_l, vp_l, sp_l, ks_l, vs_l, ss_l = [], [], [], [], [], []
    for l in range(depth):
        lam_init = 0.8 - 0.6 * math.exp(-0.3 * l)
        final = l == depth - 1

        (hq, logf, kh, hi, hz, qb, k, kb, v, vb, dz, ga, gb) = _inproj(
            xp, ng, w_bf, hgrn_lb_logits, *tabs_p, l, tm_p)
        oh, s_fin = _hgrn(hq, logf, kh, hi, s0_p, bp, sp, chunk_p, sub_p, tt_p)
        od = _flash(qb, kb, vb, lams, bp, sp, l, lam_init, tq, rb)
        xp = _outproj(oh, hz, od, dz, ga, gb, xp, hg, sg, wa, wb, wo, fg, l, lam_init, final, tm_p)
        kp_l.append(k.reshape(bp, sp, HEADS, HEAD_W))
        vp_l.append(v.reshape(bp, sp, HEADS, HEAD_W))
        sp_l.append(s_fin)

        (hq, logf, kh, hi, hz, qb, k, kb, v, vb, dz, ga, gb) = _inproj(
            xs, ng, w_bf, hgrn_lb_logits, *tabs_s, l, tm_s)
        padt = lambda a: jnp.pad(a.reshape(bs, t, W), ((0, 0), (0, t_pad - t), (0, 0))).reshape(bs * t_pad, W)
        oh, s_fin = _hgrn(padt(hq), padt(logf), padt(kh), padt(hi), state_hgrn[l], bs, t_pad, t_pad, t_pad, t_pad)
        oh = oh.reshape(bs, t_pad, W)[:, :t].reshape(bs * t, W)
        rows = lambda a: a.reshape(bs, t * HEADS, HEAD_W)
        vn = vb.reshape(bs, t, HEADS, 2 * HEAD_W)[..., :HEAD_W]
        od = _decode(rows(qb), rows(kb), rows(vn), lams, ck, cv, page_table, l, lam_init, pages)
        xs = _outproj(oh, hz, od.reshape(bs * t, W), dz, ga, gb, xs, hg, sg, wa, wb, wo, fg, l, lam_init, final, tm_s)
        ks_l.append(k.reshape(bs, t, HEADS, HEAD_W))
        vs_l.append(v.reshape(bs, t, HEADS, HEAD_W))
        ss_l.append(s_fin)

    return (xp.reshape(bp, sp, d), xs.reshape(bs, t, d), jnp.stack(kp_l), jnp.stack(vp_l), jnp.stack(sp_l),
            jnp.stack(ks_l), jnp.stack(vs_l), jnp.stack(ss_l))
```

```python
import functools
import math

import jax
import jax.numpy as jnp
from jax import lax
from jax.experimental import pallas as pl
from jax.experimental.pallas import tpu as pltpu

F32 = jnp.float32
BF16 = jnp.bfloat16

EPS = 1e-6
ROPE_THETA = 500000.0
HEADS = 4
HEAD_W = 128
SUB_W = 64
ROT_DIM = SUB_W // 4
BRANCH_W = HEADS * HEAD_W
NEG = -1e30
QK_SCALE_LOG2 = math.log2(math.e) / math.sqrt(SUB_W)

V7X_VMEM_LIMIT = 56 * 1024 * 1024
SUBLANES = 8


def _dot(a, b):
    return jnp.dot(a, b, preferred_element_type=F32)


def _dot_nt(a, b):
    return lax.dot_general(a, b, (((1,), (1,)), ((), ())), preferred_element_type=F32)


def _iota(shape, dim):
    return lax.broadcasted_iota(jnp.int32, shape, dim)


def _rms(x, g):
    return x * lax.rsqrt(jnp.mean(x * x, axis=-1, keepdims=True) + EPS) * g


def _lam(lq1_ref, lk1_ref, lq2_ref, lk2_ref, layer, lam_init):
    r = slice(layer, layer + 1)
    a = jnp.sum(lq1_ref[r, :] * lk1_ref[r, :], axis=-1, keepdims=True)
    b = jnp.sum(lq2_ref[r, :] * lk2_ref[r, :], axis=-1, keepdims=True)
    return jnp.exp(a) - jnp.exp(b) + lam_init


def _inproj_kernel(x_ref, g_ref, w_ref, lbl_ref, cos_ref, sa_ref, sb_ref, k_all_ref, v_all_ref,
                   hq_ref, log2f_ref, kh_ref, hi_ref, hz_ref, q_ref, k_ref, kb_ref, v_ref, vb_ref,
                   dz_ref, ga_ref, gb_ref, *, layer):
    del k_all_ref, v_all_ref
    h = _rms(x_ref[...], g_ref[...]).astype(BF16)
    W = BRANCH_W

    def proj(i, n=W):
        return _dot(h, w_ref[:, i * W:i * W + n])

    lbl = lbl_ref[...]
    e = jnp.exp(lbl - jnp.max(lbl, axis=0, keepdims=True))
    sm = e / jnp.sum(e, axis=0, keepdims=True)
    lb = jnp.zeros((1, W), F32)
    for i in range(1, layer + 1):
        lb = lb + sm[i:i + 1, :]

    hq_ref[...] = proj(0)
    f = lb + (1.0 - lb) * jax.nn.sigmoid(proj(1))
    log2f_ref[...] = jnp.log2(f)
    kh_ref[...] = 1.0 - f
    hi_ref[...] = proj(2)
    hz_ref[...] = proj(3).astype(BF16)

    cos, sa, sb = cos_ref[...], sa_ref[...], sb_ref[...]

    def rope(x):
        parts = []
        for hd in range(HEADS):
            xs = x[:, hd * HEAD_W:(hd + 1) * HEAD_W]
            up = pltpu.roll(xs, HEAD_W - ROT_DIM // 2, 1)
            dn = pltpu.roll(xs, ROT_DIM // 2, 1)
            parts.append(xs * cos + up * sa + dn * sb)
        return jnp.concatenate(parts, axis=1)

    def store_head_rows(ref, x):
        for hd in range(HEADS):
            ref[pl.ds(hd, x.shape[0], stride=HEADS), :] = x[:, hd * HEAD_W:(hd + 1) * HEAD_W]

    q_ref[...] = (rope(proj(4)) * QK_SCALE_LOG2).astype(BF16)
    k = rope(proj(5))
    store_head_rows(k_ref, k)
    kb_ref[...] = k.astype(BF16)
    v = proj(6)
    store_head_rows(v_ref, v)
    vb = v.astype(BF16)
    ones = jnp.ones((vb.shape[0], HEAD_W), BF16)
    vb_ref[...] = jnp.concatenate(
        [piece for hd in range(HEADS) for piece in (vb[:, hd * HEAD_W:(hd + 1) * HEAD_W], ones)], axis=1)
    dz_ref[...] = proj(7).astype(BF16)
    ga_ref[...] = proj(8, 2 * W).astype(BF16)
    gb_ref[...] = proj(10, 2 * W).astype(BF16)


def _inproj(x2d, norm_g, w_bf, lb_logits, cos_t, sa_t, sb_t, k_all, v_all, layer, tm):
    n, d = x2d.shape
    d_in = w_bf.shape[2]
    steps = n // tm
    n_tab = cos_t.shape[0] // tm
    W = BRANCH_W
    row = lambda i: (i, 0)
    const = lambda i: (0, 0)
    lay = lambda i: (layer, 0, 0)
    lay_row = lambda i: (layer * steps + i, 0)
    tab = lambda i: (i % n_tab, 0)
    f32o = jax.ShapeDtypeStruct((n, W), F32)
    bfo = lambda w: jax.ShapeDtypeStruct((n, w), BF16)
    head_rows = jax.ShapeDtypeStruct(k_all.shape, F32)
    out_shape = (f32o, f32o, f32o, f32o, bfo(W), bfo(W), head_rows, bfo(W), head_rows, bfo(2 * W),
                 bfo(W), bfo(2 * W), bfo(2 * W))
    head_spec = pl.BlockSpec((tm * HEADS, HEAD_W), lay_row)
    out_specs = tuple(head_spec if sh is head_rows else pl.BlockSpec((tm, sh.shape[1]), row) for sh in out_shape)
    any_spec = pl.BlockSpec(memory_space=pl.ANY)
    return pl.pallas_call(
        functools.partial(_inproj_kernel, layer=layer),
        grid=(steps,),
        in_specs=[pl.BlockSpec((tm, d), row), pl.BlockSpec((None, 1, d), lay), pl.BlockSpec((None, d, d_in), lay),
                  pl.BlockSpec(lb_logits.shape, const),
                  pl.BlockSpec((tm, HEAD_W), tab), pl.BlockSpec((tm, HEAD_W), tab), pl.BlockSpec((tm, HEAD_W), tab),
                  any_spec, any_spec],
        out_specs=out_specs,
        out_shape=out_shape,
        input_output_aliases={7: 6, 8: 8},
        compiler_params=pltpu.CompilerParams(dimension_semantics=("parallel",), vmem_limit_bytes=V7X_VMEM_LIMIT),
        name="inproj",
    )(x2d, norm_g, w_bf, lb_logits, cos_t, sa_t, sb_t, k_all, v_all)


def _hgrn_kernel(q_ref, g_ref, k_ref, v_ref, s0_ref, o_ref, sfin_ref, st_scr, bc_scr,
                 *, chunk, sub, n_chunks, n_seq):
    j = pl.program_id(1)

    @pl.when(j == 0)
    def _():
        for sq in range(n_seq):
            for hd in range(HEADS):
                st_scr[sq * HEADS + hd] = s0_ref[sq, hd].T

    C, SC = chunk, sub
    n_sub = C // SC
    tri = (_iota((C, C), 0) >= _iota((C, C), 1)).astype(F32).astype(BF16)
    row8 = _iota((SUBLANES, HEAD_W), 0)

    for sq in range(n_seq):
        for c in range(n_chunks):
            r0 = (sq * n_chunks + c) * C
            for hd in range(HEADS):
                si = sq * HEADS + hd
                hs = slice(hd * HEAD_W, (hd + 1) * HEAD_W)
                q = q_ref[r0:r0 + C, hs]
                g = g_ref[r0:r0 + C, hs]
                k = k_ref[r0:r0 + C, hs]
                v = v_ref[r0:r0 + C, hs]
                g0 = g.astype(BF16)
                r1 = g - g0.astype(F32)
                g1 = r1.astype(BF16)
                g2 = (r1 - g1.astype(F32)).astype(BF16)
                bc = _dot(tri, g0) + _dot(tri, g1) + _dot(tri, g2)
                bc_scr[si] = bc
                st = st_scr[si]
                vb = v.astype(BF16)

                o = _dot_nt((q * jnp.exp2(bc)).astype(BF16), st.astype(BF16))

                if n_sub > 1:
                    a_rows = [jnp.zeros((SC, C), F32)]
                    for i in range(1, n_sub):
                        bref = bc_scr[si, i * SC - 1:i * SC, :]
                        qi = q[i * SC:(i + 1) * SC] * jnp.exp2(bc[i * SC:(i + 1) * SC] - bref)
                        kt = (k[:i * SC] * jnp.exp2(bref - bc[:i * SC])).astype(BF16)
                        kt = jnp.concatenate([kt, jnp.zeros((C - i * SC, HEAD_W), BF16)], axis=0)
                        a_rows.append(_dot_nt(qi.astype(BF16), kt))
                    o = o + _dot(jnp.concatenate(a_rows, axis=0).astype(BF16), vb)

                groups = []
                for t0 in range(0, C, SUBLANES):
                    c0 = (t0 // SC) * SC
                    bg = bc[t0:t0 + SUBLANES]
                    qg = q[t0:t0 + SUBLANES]
                    acc = jnp.zeros((SUBLANES, HEAD_W), F32)
                    for s in range(c0, t0 + SUBLANES):
                        d = bg - bc_scr[si, s:s + 1, :]
                        if s >= t0:
                            d = jnp.where(row8 >= s - t0, d, NEG)
                        p = qg * jnp.exp2(d) * k_ref[r0 + s:r0 + s + 1, hs]
                        acc = acc + jnp.sum(p, axis=-1, keepdims=True) * v_ref[r0 + s:r0 + s + 1, hs]
                    groups.append(acc)
                o_ref[r0:r0 + C, hs] = o + jnp.concatenate(groups, axis=0)

                btot = bc_scr[si, C - 1:C, :]
                kdec = (k * jnp.exp2(btot - bc)).astype(BF16)
                st_scr[si] = st * jnp.exp2(btot) + _dot(v.T.astype(BF16), kdec)

    @pl.when(j == pl.num_programs(1) - 1)
    def _():
        for sq in range(n_seq):
            for hd in range(HEADS):
                sfin_ref[sq, hd] = st_scr[sq * HEADS + hd].T


def _hgrn(hq, log2f, kh, hi, s0, batch, seq, chunk, sub, tt, n_seq):
    n = batch * seq
    steps = seq // tt
    assert n_seq == 1 or steps == 1
    blk = pl.BlockSpec((n_seq * tt, BRANCH_W), lambda b, j: (b * steps + j, 0))
    st_spec = pl.BlockSpec((n_seq, HEADS, HEAD_W, HEAD_W), lambda b, j: (b, 0, 0, 0))
    return pl.pallas_call(
        functools.partial(_hgrn_kernel, chunk=chunk, sub=sub, n_chunks=tt // chunk, n_seq=n_seq),
        grid=(batch // n_seq, steps),
        in_specs=[blk, blk, blk, blk, st_spec],
        out_specs=(blk, st_spec),
        out_shape=(jax.ShapeDtypeStruct((n, BRANCH_W), F32),
                   jax.ShapeDtypeStruct((batch, HEADS, HEAD_W, HEAD_W), F32)),
        scratch_shapes=[pltpu.VMEM((n_seq * HEADS, HEAD_W, HEAD_W), F32),
                        pltpu.VMEM((n_seq * HEADS, chunk, HEAD_W), F32)],
        compiler_params=pltpu.CompilerParams(dimension_semantics=("parallel", "arbitrary"),
                                             vmem_limit_bytes=V7X_VMEM_LIMIT),
        name="hgrn",
    )(hq, log2f, kh, hi, s0)


def _stack_maps(q):
    lane = _iota(q.shape, 1)
    zero = jnp.zeros_like(q)
    return jnp.concatenate([jnp.where(lane < SUB_W, q, zero), jnp.where(lane >= SUB_W, q, zero)], axis=0)


def _flash_kernel(q_ref, k_ref, v_ref, lq1_ref, lk1_ref, lq2_ref, lk2_ref, o_ref, qs_scr, s_scr, m_scr, acc_scr,
                  *, tq, kt, rb, layer, lam_init):
    qi = pl.program_id(2)
    tk = kt * tq
    qs_scr[...] = _stack_maps(q_ref[...])
    m_scr[...] = jnp.full(m_scr.shape, NEG, F32)
    acc_scr[...] = jnp.zeros(acc_scr.shape, F32)
    n_rb = 2 * tq // rb

    def scores(r0):
        s_scr[...] = _dot_nt(qs_scr[...], k_ref[pl.ds(r0, tk), :])

    def process(r0, width, shift):
        for b in range(n_rb):
            rows = slice(b * rb, (b + 1) * rb)
            tok0 = (b * rb) % tq
            wb = width if shift is None else min(width, tok0 + rb + shift)
            s = s_scr[rows, :wb]
            if shift is not None:
                tok = _iota((rb, wb), 0) + (tok0 + shift)
                s = jnp.where(_iota((rb, wb), 1) <= tok, s, NEG)
            m_prev = m_scr[rows, :]
            m_new = jnp.maximum(m_prev, jnp.max(s, axis=-1, keepdims=True))
            p = jnp.exp2(s - m_new[:, :1]).astype(BF16)
            alpha = jnp.exp2(m_prev - m_new)
            m_scr[rows, :] = m_new
            pv = _dot(p, v_ref[pl.ds(r0, wb), :])
            acc_scr[rows, :HEAD_W] = alpha * acc_scr[rows, :HEAD_W] + pv[:, :HEAD_W]
            acc_scr[rows, HEAD_W:] = alpha * acc_scr[rows, HEAD_W:] + pv[:, HEAD_W:]

    scores(0)
    n_full = qi // kt

    def body(j, carry):
        r0 = pl.multiple_of(j * tk, tk)
        process(r0, tk, None)
        scores(r0 + tk)
        return carry

    lax.fori_loop(0, n_full, body, 0)
    r0 = pl.multiple_of(n_full * tk, tk)

    for rem in range(kt):
        @pl.when(qi % kt == rem)
        def _():
            process(r0, (rem + 1) * tq, rem * tq)

    lam = _lam(lq1_ref, lk1_ref, lq2_ref, lk2_ref, layer, lam_init)
    o = acc_scr[:, :HEAD_W] / acc_scr[:, HEAD_W:]
    o_ref[...] = o[:tq] - lam * o[tq:]


def _flash(qb, kb, vbx, lams, batch, seq, layer, lam_init, tq, kt, rb):
    n = batch * seq
    nq = seq // tq
    lam_specs = [pl.BlockSpec(a.shape, lambda b, h, i: (0, 0)) for a in lams]
    return pl.pallas_call(
        functools.partial(_flash_kernel, tq=tq, kt=kt, rb=rb, layer=layer, lam_init=lam_init),
        grid=(batch, HEADS, nq),
        in_specs=[pl.BlockSpec((tq, HEAD_W), lambda b, h, i: (b * nq + i, h)),
                  pl.BlockSpec((seq, HEAD_W), lambda b, h, i: (b, h)),
                  pl.BlockSpec((seq, 2 * HEAD_W), lambda b, h, i: (b, h))] + lam_specs,
        out_specs=pl.BlockSpec((tq, HEAD_W), lambda b, h, i: (b * nq + i, h)),
        out_shape=jax.ShapeDtypeStruct((n, BRANCH_W), F32),
        scratch_shapes=[pltpu.VMEM((2 * tq, HEAD_W), BF16), pltpu.VMEM((2 * tq, kt * tq), F32),
                        pltpu.VMEM((2 * tq, HEAD_W), F32), pltpu.VMEM((2 * tq, 2 * HEAD_W), F32)],
        compiler_params=pltpu.CompilerParams(dimension_semantics=("parallel", "parallel", "arbitrary"),
                                             vmem_limit_bytes=V7X_VMEM_LIMIT),
        name="flash",
    )(qb, kb, vbx, *lams)


def _softmax_update(blocks, m_scr, l_scr, acc_scr):
    m_prev = m_scr[...]
    m_cur = jnp.max(functools.reduce(jnp.maximum, [s for s, _ in blocks]), axis=-1, keepdims=True)
    m_new = jnp.maximum(m_prev, m_cur)
    alpha = jnp.exp2(m_prev - m_new)
    p_sum, pv = None, None
    for s, vb in blocks:
        p = jnp.exp2(s - m_new[:, :1])
        d = _dot(p.astype(BF16), vb)
        p_sum = p if p_sum is None else p_sum + p
        pv = d if pv is None else pv + d
    l_scr[...] = alpha * l_scr[...] + jnp.sum(p_sum, axis=-1, keepdims=True)
    acc_scr[...] = alpha * acc_scr[...] + pv
    m_scr[...] = m_new


def _decode_kernel(pt_ref, q_ref, kn_ref, vn_ref, lq1_ref, lk1_ref, lq2_ref, lk2_ref, *rest,
                   pages, layer, lam_init):
    del pt_ref
    k_refs, v_refs = rest[:pages], rest[pages:2 * pages]
    o_ref, m_scr, l_scr, acc_scr = rest[2 * pages:]
    g = pl.program_id(1)

    @pl.when(g == 0)
    def _():
        m_scr[...] = jnp.full(m_scr.shape, NEG, F32)
        l_scr[...] = jnp.zeros(l_scr.shape, F32)
        acc_scr[...] = jnp.zeros(acc_scr.shape, F32)

    qs = _stack_maps(q_ref[0])
    r2 = qs.shape[0]
    rows_k = k_refs[0].shape[0]
    same_head = (_iota((r2, rows_k), 0) % HEADS) == (_iota((r2, rows_k), 1) % HEADS)
    bias = jnp.where(same_head, 0.0, NEG)
    _softmax_update([(_dot_nt(qs, k_refs[p][...].astype(BF16)) + bias, v_refs[p][...].astype(BF16))
                     for p in range(pages)], m_scr, l_scr, acc_scr)

    @pl.when(g == pl.num_programs(1) - 1)
    def _():
        kn = kn_ref[0]
        rn = kn.shape[0]
        ri, ci = _iota((r2, rn), 0), _iota((r2, rn), 1)
        ri = jnp.where(ri >= rn, ri - rn, ri)
        s = jnp.where(ci // HEADS <= ri // HEADS, _dot_nt(qs, kn), NEG)
        s = jnp.where((ri % HEADS) == (ci % HEADS), s, NEG)
        _softmax_update([(s, vn_ref[0])], m_scr, l_scr, acc_scr)
        lam = _lam(lq1_ref, lk1_ref, lq2_ref, lk2_ref, layer, lam_init)
        o = acc_scr[...] / l_scr[...]
        o_ref[0] = o[:rn] - lam * o[rn:]


def _decode(q3, kn3, vn3, lams, ck, cv, page_table, layer, lam_init, pages):
    bs, rn, _ = q3.shape
    n_pages = page_table.shape[1]
    rows_k = ck.shape[2]
    new_spec = pl.BlockSpec((1, rn, HEAD_W), lambda b, g, pt: (b, 0, 0))
    lam_specs = [pl.BlockSpec(a.shape, lambda b, g, pt: (0, 0)) for a in lams]

    def page_spec(p):
        return pl.BlockSpec((None, None, rows_k, HEAD_W), lambda b, g, pt: (layer, pt[b, g * pages + p], 0, 0))

    page_specs = [page_spec(p) for p in range(pages)]
    grid_spec = pltpu.PrefetchScalarGridSpec(
        num_scalar_prefetch=1,
        grid=(bs, n_pages // pages),
        in_specs=[new_spec, new_spec, new_spec] + lam_specs + page_specs + page_specs,
        out_specs=pl.BlockSpec((1, rn, HEAD_W), lambda b, g, pt: (b, 0, 0)),
        scratch_shapes=[pltpu.VMEM((2 * rn, HEAD_W), F32), pltpu.VMEM((2 * rn, HEAD_W), F32),
                        pltpu.VMEM((2 * rn, HEAD_W), F32)],
    )
    return pl.pallas_call(
        functools.partial(_decode_kernel, pages=pages, layer=layer, lam_init=lam_init),
        grid_spec=grid_spec,
        out_shape=jax.ShapeDtypeStruct((bs, rn, HEAD_W), F32),
        compiler_params=pltpu.CompilerParams(dimension_semantics=("parallel", "arbitrary"),
                                             vmem_limit_bytes=V7X_VMEM_LIMIT),
        name="decode",
    )(page_table, q3, kn3, vn3, *lams, *([ck] * pages), *([cv] * pages))


def _outproj_kernel(oh_ref, hz_ref, od_ref, dz_ref, ga_ref, gb_ref, x_ref, hg_ref, sg_ref,
                    wa_ref, wb_ref, wo_ref, fg_ref, y_ref, *, lam_init, final):
    def headnorm(o, g):
        return jnp.concatenate([_rms(o[:, hd * HEAD_W:(hd + 1) * HEAD_W], g) for hd in range(HEADS)], axis=1)

    f32 = lambda ref: ref[...].astype(F32)
    a = headnorm(oh_ref[...], hg_ref[...]) * jax.nn.silu(f32(hz_ref))
    d = headnorm(od_ref[...], sg_ref[...]) * (1.0 - lam_init) * jax.nn.silu(f32(dz_ref))
    merged = (jax.nn.sigmoid(f32(ga_ref)) * _dot(a.astype(BF16), wa_ref[...])
              + jax.nn.sigmoid(f32(gb_ref)) * _dot(d.astype(BF16), wb_ref[...]))
    y = x_ref[...] + _dot(merged.astype(BF16), wo_ref[...])
    y_ref[...] = _rms(y, fg_ref[...]) if final else y


def _outproj(oh, hz, od, dz, ga, gb, x2d, hg, sg, wa, wb, wo, fg, layer, lam_init, final, tm):
    n, d = x2d.shape
    row = lambda i: (i, 0)
    const = lambda i: (0, 0)
    lay = lambda i: (layer, 0, 0)
    W = BRANCH_W
    return pl.pallas_call(
        functools.partial(_outproj_kernel, lam_init=lam_init, final=final),
        grid=(n // tm,),
        in_specs=[pl.BlockSpec((tm, W), row), pl.BlockSpec((tm, W), row), pl.BlockSpec((tm, W), row),
                  pl.BlockSpec((tm, W), row), pl.BlockSpec((tm, d), row), pl.BlockSpec((tm, d), row),
                  pl.BlockSpec((tm, d), row), pl.BlockSpec((None, 1, HEAD_W), lay), pl.BlockSpec((None, 1, HEAD_W), lay),
                  pl.BlockSpec((None, W, d), lay), pl.BlockSpec((None, W, d), lay), pl.BlockSpec((None, d, d), lay),
                  pl.BlockSpec((1, d), const)],
        out_specs=pl.BlockSpec((tm, d), row),
        out_shape=jax.ShapeDtypeStruct((n, d), F32),
        compiler_params=pltpu.CompilerParams(dimension_semantics=("parallel",), vmem_limit_bytes=V7X_VMEM_LIMIT),
        name="outproj",
    )(oh, hz, od, dz, ga, gb, x2d, hg, sg, wa, wb, wo, fg)


def _rope_tables(pos):
    dim = jnp.arange(HEAD_W) % SUB_W
    rot = dim < ROT_DIM
    inv_freq = jnp.where(rot, ROPE_THETA ** (-(2 * (dim % (ROT_DIM // 2))).astype(F32) / ROT_DIM), 0.0)
    ang = pos.astype(F32)[:, None] * inv_freq[None, :]
    sin = jnp.sin(ang)
    first = (dim < ROT_DIM // 2)[None, :]
    return jnp.cos(ang), jnp.where(first, -sin, 0.0), jnp.where(first, 0.0, sin)


def kernel(x_prompt, x_sample, cache_k, cache_v, state_hgrn, page_table, norm_g, w_in, hgrn_lb_logits, hgrn_norm_g,
           da_lambda_q1, da_lambda_k1, da_lambda_q2, da_lambda_k2, da_subln_g, w_branch_hgrn, w_branch_attn,
           w_out, final_norm_g):
    bp, sp, d = x_prompt.shape
    bs, t, _ = x_sample.shape
    depth = w_in.shape[0]
    page = cache_k.shape[2]
    past = page_table.shape[1] * page
    W = BRANCH_W

    tm_p = min(256, bp * sp)
    tm_s = min(256, bs * t)
    chunk_p = min(64, sp)
    sub_p = min(16, chunk_p)
    tt_p = min(256, sp)
    tq = min(1024, sp // 2)
    kt = 1
    rb = tq // 2
    assert sp % (kt * tq) == 0
    t_pad = 16
    pages = min(16, page_table.shape[1])
    seq_s = math.gcd(bs, 8)

    tabs_p = _rope_tables(jnp.arange(sp))
    tabs_s = tuple(jnp.tile(a, (tm_s // t, 1)) for a in _rope_tables(past + jnp.arange(t)))

    ck = cache_k.reshape(depth, cache_k.shape[1], page * HEADS, HEAD_W)
    cv = cache_v.reshape(depth, cache_v.shape[1], page * HEADS, HEAD_W)
    s0_p = jnp.zeros((bp, HEADS, HEAD_W, HEAD_W), F32)
    lams = (da_lambda_q1, da_lambda_k1, da_lambda_q2, da_lambda_k2)
    fg = final_norm_g.reshape(1, d)
    w_bf = w_in.astype(BF16)
    ng = norm_g.reshape(depth, 1, d)
    hg = hgrn_norm_g.reshape(depth, 1, HEAD_W)
    sg = da_subln_g.reshape(depth, 1, HEAD_W)
    wa = w_branch_hgrn.astype(BF16)
    wb = w_branch_attn.astype(BF16)
    wo = w_out.astype(BF16)

    xp = x_prompt.reshape(bp * sp, d)
    xs = x_sample.reshape(bs * t, d)
    head_rows = lambda n: jnp.zeros((depth * n * HEADS, HEAD_W), F32)
    kp, vp, ks, vs = head_rows(bp * sp), head_rows(bp * sp), head_rows(bs * t), head_rows(bs * t)
    sp_l, ss_l = [], []
    for l in range(depth):
        lam_init = 0.8 - 0.6 * math.exp(-0.3 * l)
        final = l == depth - 1

        (hq, log2f, kh, hi, hz, qb, kp, kb, vp, vb, dz, ga, gb) = _inproj(
            xp, ng, w_bf, hgrn_lb_logits, *tabs_p, kp, vp, l, tm_p)
        oh, s_fin = _hgrn(hq, log2f, kh, hi, s0_p, bp, sp, chunk_p, sub_p, tt_p, 1)
        od = _flash(qb, kb, vb, lams, bp, sp, l, lam_init, tq, kt, rb)
        xp = _outproj(oh, hz, od, dz, ga, gb, xp, hg, sg, wa, wb, wo, fg, l, lam_init, final, tm_p)
        sp_l.append(s_fin)

        (hq, log2f, kh, hi, hz, qb, ks, kb, vs, vb, dz, ga, gb) = _inproj(
            xs, ng, w_bf, hgrn_lb_logits, *tabs_s, ks, vs, l, tm_s)
        padt = lambda a: jnp.pad(a.reshape(bs, t, W), ((0, 0), (0, t_pad - t), (0, 0))).reshape(bs * t_pad, W)
        oh, s_fin = _hgrn(padt(hq), padt(log2f), padt(kh), padt(hi), state_hgrn[l], bs, t_pad, t_pad, t_pad, t_pad,
                          seq_s)
        oh = oh.reshape(bs, t_pad, W)[:, :t].reshape(bs * t, W)
        rows = lambda a: a.reshape(bs, t * HEADS, HEAD_W)
        vn = vb.reshape(bs, t, HEADS, 2 * HEAD_W)[..., :HEAD_W]
        od = _decode(rows(qb), rows(kb), rows(vn), lams, ck, cv, page_table, l, lam_init, pages)
        xs = _outproj(oh, hz, od.reshape(bs * t, W), dz, ga, gb, xs, hg, sg, wa, wb, wo, fg, l, lam_init, final, tm_s)
        ss_l.append(s_fin)

    shape_p = (depth, bp, sp, HEADS, HEAD_W)
    shape_s = (depth, bs, t, HEADS, HEAD_W)
    return (xp.reshape(bp, sp, d), xs.reshape(bs, t, d), kp.reshape(shape_p), vp.reshape(shape_p), jnp.stack(sp_l),
            ks.reshape(shape_s), vs.reshape(shape_s), jnp.stack(ss_l))
```

```python
import functools
import math

import jax
import jax.numpy as jnp
from jax import lax
from jax.experimental import pallas as pl
from jax.experimental.pallas import tpu as pltpu

F32 = jnp.float32
BF16 = jnp.bfloat16

EPS = 1e-6
ROPE_THETA = 500000.0
HEADS = 4
HEAD_W = 128
SUB_W = 64
ROT_DIM = SUB_W // 4
BRANCH_W = HEADS * HEAD_W
NEG = -1e30
QK_SCALE_LOG2 = math.log2(math.e) / math.sqrt(SUB_W)

V7X_VMEM_LIMIT = 56 * 1024 * 1024
SUBLANES = 8


def _dot(a, b):
    return jnp.dot(a, b, preferred_element_type=F32)


def _dot_nt(a, b):
    return lax.dot_general(a, b, (((1,), (1,)), ((), ())), preferred_element_type=F32)


def _iota(shape, dim):
    return lax.broadcasted_iota(jnp.int32, shape, dim)


def _rms(x, g):
    return x * lax.rsqrt(jnp.mean(x * x, axis=-1, keepdims=True) + EPS) * g


def _lam(lq1_ref, lk1_ref, lq2_ref, lk2_ref, layer, lam_init):
    r = slice(layer, layer + 1)
    a = jnp.sum(lq1_ref[r, :] * lk1_ref[r, :], axis=-1, keepdims=True)
    b = jnp.sum(lq2_ref[r, :] * lk2_ref[r, :], axis=-1, keepdims=True)
    return jnp.exp(a) - jnp.exp(b) + lam_init


def _inproj_kernel(x_ref, g_ref, w_ref, lbl_ref, cos_ref, sa_ref, sb_ref, k_all_ref, v_all_ref,
                   hq_ref, log2f_ref, kh_ref, hi_ref, hz_ref, q_ref, k_ref, kb_ref, v_ref, vb_ref,
                   dz_ref, ga_ref, gb_ref, *, layer):
    del k_all_ref, v_all_ref
    h = _rms(x_ref[...], g_ref[...]).astype(BF16)
    W = BRANCH_W

    def proj(i, n=W):
        return _dot(h, w_ref[:, i * W:i * W + n])

    lbl = lbl_ref[...]
    e = jnp.exp(lbl - jnp.max(lbl, axis=0, keepdims=True))
    sm = e / jnp.sum(e, axis=0, keepdims=True)
    lb = jnp.zeros((1, W), F32)
    for i in range(1, layer + 1):
        lb = lb + sm[i:i + 1, :]

    hq_ref[...] = proj(0)
    f = lb + (1.0 - lb) * jax.nn.sigmoid(proj(1))
    log2f_ref[...] = jnp.log2(f)
    kh_ref[...] = 1.0 - f
    hi_ref[...] = proj(2)
    hz_ref[...] = proj(3).astype(BF16)

    cos, sa, sb = cos_ref[...], sa_ref[...], sb_ref[...]

    def rope(x):
        parts = []
        for hd in range(HEADS):
            xs = x[:, hd * HEAD_W:(hd + 1) * HEAD_W]
            up = pltpu.roll(xs, HEAD_W - ROT_DIM // 2, 1)
            dn = pltpu.roll(xs, ROT_DIM // 2, 1)
            parts.append(xs * cos + up * sa + dn * sb)
        return jnp.concatenate(parts, axis=1)

    def store_head_rows(ref, x):
        for hd in range(HEADS):
            ref[pl.ds(hd, x.shape[0], stride=HEADS), :] = x[:, hd * HEAD_W:(hd + 1) * HEAD_W]

    q_ref[...] = (rope(proj(4)) * QK_SCALE_LOG2).astype(BF16)
    k = rope(proj(5))
    store_head_rows(k_ref, k)
    kb_ref[...] = k.astype(BF16)
    v = proj(6)
    store_head_rows(v_ref, v)
    vb = v.astype(BF16)
    ones = jnp.ones((vb.shape[0], HEAD_W), BF16)
    vb_ref[...] = jnp.concatenate(
        [piece for hd in range(HEADS) for piece in (vb[:, hd * HEAD_W:(hd + 1) * HEAD_W], ones)], axis=1)
    dz_ref[...] = proj(7).astype(BF16)
    ga_ref[...] = proj(8, 2 * W).astype(BF16)
    gb_ref[...] = proj(10, 2 * W).astype(BF16)


def _inproj(x2d, norm_g, w_bf, lb_logits, cos_t, sa_t, sb_t, k_all, v_all, layer, tm):
    n, d = x2d.shape
    d_in = w_bf.shape[2]
    steps = n // tm
    n_tab = cos_t.shape[0] // tm
    W = BRANCH_W
    row = lambda i: (i, 0)
    const = lambda i: (0, 0)
    lay = lambda i: (layer, 0, 0)
    lay_row = lambda i: (layer * steps + i, 0)
    tab = lambda i: (i % n_tab, 0)
    f32o = jax.ShapeDtypeStruct((n, W), F32)
    bfo = lambda w: jax.ShapeDtypeStruct((n, w), BF16)
    head_rows = jax.ShapeDtypeStruct(k_all.shape, F32)
    out_shape = (f32o, f32o, f32o, f32o, bfo(W), bfo(W), head_rows, bfo(W), head_rows, bfo(2 * W),
                 bfo(W), bfo(2 * W), bfo(2 * W))
    head_spec = pl.BlockSpec((tm * HEADS, HEAD_W), lay_row)
    out_specs = tuple(head_spec if sh is head_rows else pl.BlockSpec((tm, sh.shape[1]), row) for sh in out_shape)
    any_spec = pl.BlockSpec(memory_space=pl.ANY)
    return pl.pallas_call(
        functools.partial(_inproj_kernel, layer=layer),
        grid=(steps,),
        in_specs=[pl.BlockSpec((tm, d), row), pl.BlockSpec((None, 1, d), lay), pl.BlockSpec((None, d, d_in), lay),
                  pl.BlockSpec(lb_logits.shape, const),
                  pl.BlockSpec((tm, HEAD_W), tab), pl.BlockSpec((tm, HEAD_W), tab), pl.BlockSpec((tm, HEAD_W), tab),
                  any_spec, any_spec],
        out_specs=out_specs,
        out_shape=out_shape,
        input_output_aliases={7: 6, 8: 8},
        compiler_params=pltpu.CompilerParams(dimension_semantics=("parallel",), vmem_limit_bytes=V7X_VMEM_LIMIT),
        name="inproj",
    )(x2d, norm_g, w_bf, lb_logits, cos_t, sa_t, sb_t, k_all, v_all)


def _hgrn_kernel(q_ref, g_ref, k_ref, v_ref, s0_ref, o_ref, sfin_ref, st_scr, bc_scr, upd_scr,
                 *, chunk, sub, n_chunks, n_seq):
    j = pl.program_id(1)

    @pl.when(j == 0)
    def _():
        for sq in range(n_seq):
            for hd in range(HEADS):
                st_scr[sq * HEADS + hd] = s0_ref[sq, hd].T

    C, SC = chunk, sub
    n_sub = C // SC
    tri = (_iota((C, C), 0) >= _iota((C, C), 1)).astype(F32).astype(BF16)
    row8 = _iota((SUBLANES, HEAD_W), 0)

    units = [((sq * n_chunks + c) * C, slice(hd * HEAD_W, (hd + 1) * HEAD_W),
              (sq * n_chunks + c) * HEADS + hd, sq * HEADS + hd)
             for sq in range(n_seq) for c in range(n_chunks) for hd in range(HEADS)]


    for r0, hs, ci, _ in units:
        g = g_ref[r0:r0 + C, hs]
        g0 = g.astype(BF16)
        r1 = g - g0.astype(F32)
        g1 = r1.astype(BF16)
        g2 = (r1 - g1.astype(F32)).astype(BF16)
        bc_scr[ci] = _dot(tri, g0) + _dot(tri, g1) + _dot(tri, g2)

    for r0, hs, ci, _ in units:
        btot = bc_scr[ci, C - 1:C, :]
        kdec = (k_ref[r0:r0 + C, hs] * jnp.exp2(btot - bc_scr[ci])).astype(BF16)
        upd_scr[ci] = _dot(v_ref[r0:r0 + C, hs].T.astype(BF16), kdec)

    for r0, hs, ci, si in units:
        st = st_scr[si]
        qdec = (q_ref[r0:r0 + C, hs] * jnp.exp2(bc_scr[ci])).astype(BF16)
        o_ref[r0:r0 + C, hs] = _dot_nt(qdec, st.astype(BF16))
        st_scr[si] = st * jnp.exp2(bc_scr[ci, C - 1:C, :]) + upd_scr[ci]

    if n_sub > 1:
        for r0, hs, ci, _ in units:
            q = q_ref[r0:r0 + C, hs]
            k = k_ref[r0:r0 + C, hs]
            bc = bc_scr[ci]
            a_rows = [jnp.zeros((SC, C), F32)]
            for i in range(1, n_sub):
                bref = bc_scr[ci, i * SC - 1:i * SC, :]
                qi = q[i * SC:(i + 1) * SC] * jnp.exp2(bc[i * SC:(i + 1) * SC] - bref)
                kt = (k[:i * SC] * jnp.exp2(bref - bc[:i * SC])).astype(BF16)
                kt = jnp.concatenate([kt, jnp.zeros((C - i * SC, HEAD_W), BF16)], axis=0)
                a_rows.append(_dot_nt(qi.astype(BF16), kt))
            o_ref[r0:r0 + C, hs] += _dot(jnp.concatenate(a_rows, axis=0).astype(BF16),
                                         v_ref[r0:r0 + C, hs].astype(BF16))

    for r0, hs, ci, _ in units:
        for t0 in range(0, C, SUBLANES):
            c0 = (t0 // SC) * SC
            bg = bc_scr[ci, t0:t0 + SUBLANES, :]
            qg = q_ref[r0 + t0:r0 + t0 + SUBLANES, hs]
            acc = jnp.zeros((SUBLANES, HEAD_W), F32)
            for s in range(c0, t0 + SUBLANES):
                d = bg - bc_scr[ci, s:s + 1, :]
                if s >= t0:
                    d = jnp.where(row8 >= s - t0, d, NEG)
                p = qg * jnp.exp2(d) * k_ref[r0 + s:r0 + s + 1, hs]
                acc = acc + jnp.sum(p, axis=-1, keepdims=True) * v_ref[r0 + s:r0 + s + 1, hs]
            o_ref[r0 + t0:r0 + t0 + SUBLANES, hs] += acc

    @pl.when(j == pl.num_programs(1) - 1)
    def _():
        for sq in range(n_seq):
            for hd in range(HEADS):
                sfin_ref[sq, hd] = st_scr[sq * HEADS + hd].T


def _hgrn(hq, log2f, kh, hi, s0, batch, seq, chunk, sub, tt, n_seq):
    n = batch * seq
    steps = seq // tt
    assert n_seq == 1 or steps == 1
    blk = pl.BlockSpec((n_seq * tt, BRANCH_W), lambda b, j: (b * steps + j, 0))
    st_spec = pl.BlockSpec((n_seq, HEADS, HEAD_W, HEAD_W), lambda b, j: (b, 0, 0, 0))
    return pl.pallas_call(
        functools.partial(_hgrn_kernel, chunk=chunk, sub=sub, n_chunks=tt // chunk, n_seq=n_seq),
        grid=(batch // n_seq, steps),
        in_specs=[blk, blk, blk, blk, st_spec],
        out_specs=(blk, st_spec),
        out_shape=(jax.ShapeDtypeStruct((n, BRANCH_W), F32),
                   jax.ShapeDtypeStruct((batch, HEADS, HEAD_W, HEAD_W), F32)),
        scratch_shapes=[pltpu.VMEM((n_seq * HEADS, HEAD_W, HEAD_W), F32),
                        pltpu.VMEM((n_seq * (tt // chunk) * HEADS, chunk, HEAD_W), F32),
                        pltpu.VMEM((n_seq * (tt // chunk) * HEADS, HEAD_W, HEAD_W), F32)],
        compiler_params=pltpu.CompilerParams(dimension_semantics=("parallel", "arbitrary"),
                                             vmem_limit_bytes=V7X_VMEM_LIMIT),
        name="hgrn",
    )(hq, log2f, kh, hi, s0)


def _stack_maps(q):
    lane = _iota(q.shape, 1)
    zero = jnp.zeros_like(q)
    return jnp.concatenate([jnp.where(lane < SUB_W, q, zero), jnp.where(lane >= SUB_W, q, zero)], axis=0)


def _flash_kernel(q_ref, k_ref, v_ref, lq1_ref, lk1_ref, lq2_ref, lk2_ref, o_ref, qs_scr, s_scr, m_scr, acc_scr,
                  *, tq, kt, rb, layer, lam_init):
    qi = pl.program_id(2)
    tk = kt * tq
    qs_scr[...] = _stack_maps(q_ref[...])
    m_scr[...] = jnp.full(m_scr.shape, NEG, F32)
    acc_scr[...] = jnp.zeros(acc_scr.shape, F32)
    n_rb = 2 * tq // rb

    def scores(r0):
        s_scr[...] = _dot_nt(qs_scr[...], k_ref[pl.ds(r0, tk), :])

    def process(r0, width, shift):
        for b in range(n_rb):
            rows = slice(b * rb, (b + 1) * rb)
            tok0 = (b * rb) % tq
            wb = width if shift is None else min(width, tok0 + rb + shift)
            s = s_scr[rows, :wb]
            if shift is not None:
                tok = _iota((rb, wb), 0) + (tok0 + shift)
                s = jnp.where(_iota((rb, wb), 1) <= tok, s, NEG)
            m_prev = m_scr[rows, :]
            m_new = jnp.maximum(m_prev, jnp.max(s, axis=-1, keepdims=True))
            p = jnp.exp2(s - m_new[:, :1]).astype(BF16)
            alpha = jnp.exp2(m_prev - m_new)
            m_scr[rows, :] = m_new
            pv = _dot(p, v_ref[pl.ds(r0, wb), :])
            acc_scr[rows, :HEAD_W] = alpha * acc_scr[rows, :HEAD_W] + pv[:, :HEAD_W]
            acc_scr[rows, HEAD_W:] = alpha * acc_scr[rows, HEAD_W:] + pv[:, HEAD_W:]

    scores(0)
    n_full = qi // kt

    def body(j, carry):
        r0 = pl.multiple_of(j * tk, tk)
        process(r0, tk, None)
        scores(r0 + tk)
        return carry

    lax.fori_loop(0, n_full, body, 0)
    r0 = pl.multiple_of(n_full * tk, tk)

    for rem in range(kt):
        @pl.when(qi % kt == rem)
        def _():
            process(r0, (rem + 1) * tq, rem * tq)

    lam = _lam(lq1_ref, lk1_ref, lq2_ref, lk2_ref, layer, lam_init)
    o = acc_scr[:, :HEAD_W] / acc_scr[:, HEAD_W:]
    o_ref[...] = o[:tq] - lam * o[tq:]


def _flash(qb, kb, vbx, lams, batch, seq, layer, lam_init, tq, kt, rb):
    n = batch * seq
    nq = seq // tq
    lam_specs = [pl.BlockSpec(a.shape, lambda b, h, i: (0, 0)) for a in lams]
    return pl.pallas_call(
        functools.partial(_flash_kernel, tq=tq, kt=kt, rb=rb, layer=layer, lam_init=lam_init),
        grid=(batch, HEADS, nq),
        in_specs=[pl.BlockSpec((tq, HEAD_W), lambda b, h, i: (b * nq + i, h)),
                  pl.BlockSpec((seq, HEAD_W), lambda b, h, i: (b, h)),
                  pl.BlockSpec((seq, 2 * HEAD_W), lambda b, h, i: (b, h))] + lam_specs,
        out_specs=pl.BlockSpec((tq, HEAD_W), lambda b, h, i: (b * nq + i, h)),
        out_shape=jax.ShapeDtypeStruct((n, BRANCH_W), F32),
        scratch_shapes=[pltpu.VMEM((2 * tq, HEAD_W), BF16), pltpu.VMEM((2 * tq, kt * tq), F32),
                        pltpu.VMEM((2 * tq, HEAD_W), F32), pltpu.VMEM((2 * tq, 2 * HEAD_W), F32)],
        compiler_params=pltpu.CompilerParams(dimension_semantics=("parallel", "parallel", "arbitrary"),
                                             vmem_limit_bytes=V7X_VMEM_LIMIT),
        name="flash",
    )(qb, kb, vbx, *lams)


def _softmax_update(blocks, m_scr, l_scr, acc_scr):
    m_prev = m_scr[...]
    m_cur = jnp.max(functools.reduce(jnp.maximum, [s for s, _ in blocks]), axis=-1, keepdims=True)
    m_new = jnp.maximum(m_prev, m_cur)
    alpha = jnp.exp2(m_prev - m_new)
    p_sum, pv = None, None
    for s, vb in blocks:
        p = jnp.exp2(s - m_new[:, :1])
        d = _dot(p.astype(BF16), vb)
        p_sum = p if p_sum is None else p_sum + p
        pv = d if pv is None else pv + d
    l_scr[...] = alpha * l_scr[...] + jnp.sum(p_sum, axis=-1, keepdims=True)
    acc_scr[...] = alpha * acc_scr[...] + pv
    m_scr[...] = m_new


def _decode_kernel(pt_ref, q_ref, kn_ref, vn_ref, lq1_ref, lk1_ref, lq2_ref, lk2_ref, *rest,
                   pages, layer, lam_init):
    del pt_ref
    k_refs, v_refs = rest[:pages], rest[pages:2 * pages]
    o_ref, m_scr, l_scr, acc_scr = rest[2 * pages:]
    g = pl.program_id(1)

    @pl.when(g == 0)
    def _():
        m_scr[...] = jnp.full(m_scr.shape, NEG, F32)
        l_scr[...] = jnp.zeros(l_scr.shape, F32)
        acc_scr[...] = jnp.zeros(acc_scr.shape, F32)

    qs = _stack_maps(q_ref[0])
    r2 = qs.shape[0]
    rows_k = k_refs[0].shape[0]
    same_head = (_iota((r2, rows_k), 0) % HEADS) == (_iota((r2, rows_k), 1) % HEADS)
    bias = jnp.where(same_head, 0.0, NEG)
    _softmax_update([(_dot_nt(qs, k_refs[p][...].astype(BF16)) + bias, v_refs[p][...].astype(BF16))
                     for p in range(pages)], m_scr, l_scr, acc_scr)

    @pl.when(g == pl.num_programs(1) - 1)
    def _():
        kn = kn_ref[0]
        rn = kn.shape[0]
        ri, ci = _iota((r2, rn), 0), _iota((r2, rn), 1)
        ri = jnp.where(ri >= rn, ri - rn, ri)
        s = jnp.where(ci // HEADS <= ri // HEADS, _dot_nt(qs, kn), NEG)
        s = jnp.where((ri % HEADS) == (ci % HEADS), s, NEG)
        _softmax_update([(s, vn_ref[0])], m_scr, l_scr, acc_scr)
        lam = _lam(lq1_ref, lk1_ref, lq2_ref, lk2_ref, layer, lam_init)
        o = acc_scr[...] / l_scr[...]
        o_ref[0] = o[:rn] - lam * o[rn:]


def _decode(q3, kn3, vn3, lams, ck, cv, page_rows, layer, lam_init, pages):
    bs, rn, _ = q3.shape
    n_pages = page_rows.shape[1]
    rows_k = ck.shape[1]
    new_spec = pl.BlockSpec((1, rn, HEAD_W), lambda b, g, pt: (b, 0, 0))
    lam_specs = [pl.BlockSpec(a.shape, lambda b, g, pt: (0, 0)) for a in lams]

    def page_spec(p):
        return pl.BlockSpec((None, rows_k, HEAD_W), lambda b, g, pt: (pt[b, g * pages + p], 0, 0))

    page_specs = [page_spec(p) for p in range(pages)]
    grid_spec = pltpu.PrefetchScalarGridSpec(
        num_scalar_prefetch=1,
        grid=(bs, n_pages // pages),
        in_specs=[new_spec, new_spec, new_spec] + lam_specs + page_specs + page_specs,
        out_specs=pl.BlockSpec((1, rn, HEAD_W), lambda b, g, pt: (b, 0, 0)),
        scratch_shapes=[pltpu.VMEM((2 * rn, HEAD_W), F32), pltpu.VMEM((2 * rn, HEAD_W), F32),
                        pltpu.VMEM((2 * rn, HEAD_W), F32)],
    )
    return pl.pallas_call(
        functools.partial(_decode_kernel, pages=pages, layer=layer, lam_init=lam_init),
        grid_spec=grid_spec,
        out_shape=jax.ShapeDtypeStruct((bs, rn, HEAD_W), F32),
        compiler_params=pltpu.CompilerParams(dimension_semantics=("parallel", "arbitrary"),
                                             vmem_limit_bytes=V7X_VMEM_LIMIT),
        name="decode",
    )(page_rows, q3, kn3, vn3, *lams, *([ck] * pages), *([cv] * pages))


def _outproj_kernel(oh_ref, hz_ref, od_ref, dz_ref, ga_ref, gb_ref, x_ref, hg_ref, sg_ref,
                    wa_ref, wb_ref, wo_ref, fg_ref, y_ref, *, lam_init, final):
    def headnorm(o, g):
        return jnp.concatenate([_rms(o[:, hd * HEAD_W:(hd + 1) * HEAD_W], g) for hd in range(HEADS)], axis=1)

    f32 = lambda ref: ref[...].astype(F32)
    a = headnorm(oh_ref[...], hg_ref[...]) * jax.nn.silu(f32(hz_ref))
    d = headnorm(od_ref[...], sg_ref[...]) * (1.0 - lam_init) * jax.nn.silu(f32(dz_ref))
    merged = (jax.nn.sigmoid(f32(ga_ref)) * _dot(a.astype(BF16), wa_ref[...])
              + jax.nn.sigmoid(f32(gb_ref)) * _dot(d.astype(BF16), wb_ref[...]))
    y = x_ref[...] + _dot(merged.astype(BF16), wo_ref[...])
    y_ref[...] = _rms(y, fg_ref[...]) if final else y


def _outproj(oh, hz, od, dz, ga, gb, x2d, hg, sg, wa, wb, wo, fg, layer, lam_init, final, tm):
    n, d = x2d.shape
    row = lambda i: (i, 0)
    const = lambda i: (0, 0)
    lay = lambda i: (layer, 0, 0)
    W = BRANCH_W
    return pl.pallas_call(
        functools.partial(_outproj_kernel, lam_init=lam_init, final=final),
        grid=(n // tm,),
        in_specs=[pl.BlockSpec((tm, W), row), pl.BlockSpec((tm, W), row), pl.BlockSpec((tm, W), row),
                  pl.BlockSpec((tm, W), row), pl.BlockSpec((tm, d), row), pl.BlockSpec((tm, d), row),
                  pl.BlockSpec((tm, d), row), pl.BlockSpec((None, 1, HEAD_W), lay), pl.BlockSpec((None, 1, HEAD_W), lay),
                  pl.BlockSpec((None, W, d), lay), pl.BlockSpec((None, W, d), lay), pl.BlockSpec((None, d, d), lay),
                  pl.BlockSpec((1, d), const)],
        out_specs=pl.BlockSpec((tm, d), row),
        out_shape=jax.ShapeDtypeStruct((n, d), F32),
        compiler_params=pltpu.CompilerParams(dimension_semantics=("parallel",), vmem_limit_bytes=V7X_VMEM_LIMIT),
        name="outproj",
    )(oh, hz, od, dz, ga, gb, x2d, hg, sg, wa, wb, wo, fg)


def _rope_tables(pos):
    dim = jnp.arange(HEAD_W) % SUB_W
    rot = dim < ROT_DIM
    inv_freq = jnp.where(rot, ROPE_THETA ** (-(2 * (dim % (ROT_DIM // 2))).astype(F32) / ROT_DIM), 0.0)
    ang = pos.astype(F32)[:, None] * inv_freq[None, :]
    sin = jnp.sin(ang)
    first = (dim < ROT_DIM // 2)[None, :]
    return jnp.cos(ang), jnp.where(first, -sin, 0.0), jnp.where(first, 0.0, sin)


def kernel(x_prompt, x_sample, cache_k, cache_v, state_hgrn, page_table, norm_g, w_in, hgrn_lb_logits, hgrn_norm_g,
           da_lambda_q1, da_lambda_k1, da_lambda_q2, da_lambda_k2, da_subln_g, w_branch_hgrn, w_branch_attn,
           w_out, final_norm_g):
    bp, sp, d = x_prompt.shape
    bs, t, _ = x_sample.shape
    depth = w_in.shape[0]
    page = cache_k.shape[2]
    past = page_table.shape[1] * page
    W = BRANCH_W

    tm_p = min(256, bp * sp)
    tm_s = min(256, bs * t)
    chunk_p = min(64, sp)
    sub_p = min(16, chunk_p)
    tt_p = min(256, sp)
    tq = min(1024, sp // 2)
    kt = 1
    rb = tq // 2
    assert sp % (kt * tq) == 0
    t_pad = 16
    pages = min(32, page_table.shape[1])
    seq_s = math.gcd(bs, 8)

    tabs_p = _rope_tables(jnp.arange(sp))
    tabs_s = tuple(jnp.tile(a, (tm_s // t, 1)) for a in _rope_tables(past + jnp.arange(t)))

    pool = cache_k.shape[1]
    ck = cache_k.reshape(depth * pool, page * HEADS, HEAD_W)
    cv = cache_v.reshape(depth * pool, page * HEADS, HEAD_W)
    s0_p = jnp.zeros((bp, HEADS, HEAD_W, HEAD_W), F32)
    lams = (da_lambda_q1, da_lambda_k1, da_lambda_q2, da_lambda_k2)
    fg = final_norm_g.reshape(1, d)
    w_bf = w_in.astype(BF16)
    ng = norm_g.reshape(depth, 1, d)
    hg = hgrn_norm_g.reshape(depth, 1, HEAD_W)
    sg = da_subln_g.reshape(depth, 1, HEAD_W)
    wa = w_branch_hgrn.astype(BF16)
    wb = w_branch_attn.astype(BF16)
    wo = w_out.astype(BF16)

    xp = x_prompt.reshape(bp * sp, d)
    xs = x_sample.reshape(bs * t, d)
    head_rows = lambda n: jnp.zeros((depth * n * HEADS, HEAD_W), F32)
    kp, vp, ks, vs = head_rows(bp * sp), head_rows(bp * sp), head_rows(bs * t), head_rows(bs * t)
    sp_l, ss_l = [], []
    for l in range(depth):
        lam_init = 0.8 - 0.6 * math.exp(-0.3 * l)
        final = l == depth - 1

        (hq, log2f, kh, hi, hz, qb, kp, kb, vp, vb, dz, ga, gb) = _inproj(
            xp, ng, w_bf, hgrn_lb_logits, *tabs_p, kp, vp, l, tm_p)
        oh, s_fin = _hgrn(hq, log2f, kh, hi, s0_p, bp, sp, chunk_p, sub_p, tt_p, 1)
        od = _flash(qb, kb, vb, lams, bp, sp, l, lam_init, tq, kt, rb)
        xp = _outproj(oh, hz, od, dz, ga, gb, xp, hg, sg, wa, wb, wo, fg, l, lam_init, final, tm_p)
        sp_l.append(s_fin)

        (hq, log2f, kh, hi, hz, qb, ks, kb, vs, vb, dz, ga, gb) = _inproj(
            xs, ng, w_bf, hgrn_lb_logits, *tabs_s, ks, vs, l, tm_s)
        padt = lambda a: jnp.pad(a.reshape(bs, t, W), ((0, 0), (0, t_pad - t), (0, 0))).reshape(bs * t_pad, W)
        oh, s_fin = _hgrn(padt(hq), padt(log2f), padt(kh), padt(hi), state_hgrn[l], bs, t_pad, t_pad, t_pad, t_pad,
                          seq_s)
        oh = oh.reshape(bs, t_pad, W)[:, :t].reshape(bs * t, W)
        rows = lambda a: a.reshape(bs, t * HEADS, HEAD_W)
        vn = vb.reshape(bs, t, HEADS, 2 * HEAD_W)[..., :HEAD_W]
        od = _decode(rows(qb), rows(kb), rows(vn), lams, ck, cv, page_table + l * pool, l, lam_init, pages)
        xs = _outproj(oh, hz, od.reshape(bs * t, W), dz, ga, gb, xs, hg, sg, wa, wb, wo, fg, l, lam_init, final, tm_s)
        ss_l.append(s_fin)

    shape_p = (depth, bp, sp, HEADS, HEAD_W)
    shape_s = (depth, bs, t, HEADS, HEAD_W)
    return (xp.reshape(bp, sp, d), xs.reshape(bs, t, d), kp.reshape(shape_p), vp.reshape(shape_p), jnp.stack(sp_l),
            ks.reshape(shape_s), vs.reshape(shape_s), jnp.stack(ss_l))
```

```python
import functools
import math

import jax
import jax.numpy as jnp
from jax import lax
from jax.experimental import pallas as pl
from jax.experimental.pallas import tpu as pltpu

F32 = jnp.float32
BF16 = jnp.bfloat16

EPS = 1e-6
ROPE_THETA = 500000.0
HEADS = 4
HEAD_W = 128
SUB_W = 64
ROT_DIM = SUB_W // 4
BRANCH_W = HEADS * HEAD_W
NEG = -1e30
QK_SCALE_LOG2 = math.log2(math.e) / math.sqrt(SUB_W)

V7X_VMEM_LIMIT = 56 * 1024 * 1024
SUBLANES = 8


def _dot(a, b):
    return jnp.dot(a, b, preferred_element_type=F32)


def _dot_nt(a, b):
    return lax.dot_general(a, b, (((1,), (1,)), ((), ())), preferred_element_type=F32)


def _iota(shape, dim):
    return lax.broadcasted_iota(jnp.int32, shape, dim)


def _rms(x, g):
    return x * lax.rsqrt(jnp.mean(x * x, axis=-1, keepdims=True) + EPS) * g


def _lam(lq1_ref, lk1_ref, lq2_ref, lk2_ref, layer, lam_init):
    r = slice(layer, layer + 1)
    a = jnp.sum(lq1_ref[r, :] * lk1_ref[r, :], axis=-1, keepdims=True)
    b = jnp.sum(lq2_ref[r, :] * lk2_ref[r, :], axis=-1, keepdims=True)
    return jnp.exp(a) - jnp.exp(b) + lam_init


def _inproj_kernel(x_ref, g_ref, w_ref, lbl_ref, cos_ref, sa_ref, sb_ref, k_all_ref, v_all_ref,
                   hq_ref, log2f_ref, kh_ref, hi_ref, hz_ref, q_ref, k_ref, kb_ref, v_ref, vb_ref,
                   dz_ref, ga_ref, gb_ref, *, layer):
    del k_all_ref, v_all_ref
    h = _rms(x_ref[...], g_ref[...]).astype(BF16)
    W = BRANCH_W

    def proj(i, n=W):
        return _dot(h, w_ref[:, i * W:i * W + n])

    lbl = lbl_ref[...]
    e = jnp.exp(lbl - jnp.max(lbl, axis=0, keepdims=True))
    sm = e / jnp.sum(e, axis=0, keepdims=True)
    lb = jnp.zeros((1, W), F32)
    for i in range(1, layer + 1):
        lb = lb + sm[i:i + 1, :]

    hq_ref[...] = proj(0)
    f = lb + (1.0 - lb) * jax.nn.sigmoid(proj(1))
    log2f_ref[...] = jnp.log2(f)
    kh_ref[...] = 1.0 - f
    hi_ref[...] = proj(2)
    hz_ref[...] = jax.nn.silu(proj(3)).astype(BF16)

    cos, sa, sb = cos_ref[...], sa_ref[...], sb_ref[...]

    def rope(x):
        parts = []
        for hd in range(HEADS):
            xs = x[:, hd * HEAD_W:(hd + 1) * HEAD_W]
            up = pltpu.roll(xs, HEAD_W - ROT_DIM // 2, 1)
            dn = pltpu.roll(xs, ROT_DIM // 2, 1)
            parts.append(xs * cos + up * sa + dn * sb)
        return jnp.concatenate(parts, axis=1)

    def store_head_rows(ref, x):
        for hd in range(HEADS):
            ref[pl.ds(hd, x.shape[0], stride=HEADS), :] = x[:, hd * HEAD_W:(hd + 1) * HEAD_W]

    q_ref[...] = (rope(proj(4)) * QK_SCALE_LOG2).astype(BF16)
    k = rope(proj(5))
    store_head_rows(k_ref, k)
    kb_ref[...] = k.astype(BF16)
    v = proj(6)
    store_head_rows(v_ref, v)
    vb = v.astype(BF16)
    ones = jnp.ones((vb.shape[0], HEAD_W), BF16)
    vb_ref[...] = jnp.concatenate(
        [piece for hd in range(HEADS) for piece in (vb[:, hd * HEAD_W:(hd + 1) * HEAD_W], ones)], axis=1)
    dz_ref[...] = jax.nn.silu(proj(7)).astype(BF16)
    ga_ref[...] = jax.nn.sigmoid(proj(8, 2 * W)).astype(BF16)
    gb_ref[...] = jax.nn.sigmoid(proj(10, 2 * W)).astype(BF16)


def _inproj(x2d, norm_g, w_bf, lb_logits, cos_t, sa_t, sb_t, k_all, v_all, layer, tm):
    n, d = x2d.shape
    d_in = w_bf.shape[2]
    steps = n // tm
    n_tab = cos_t.shape[0] // tm
    W = BRANCH_W
    row = lambda i: (i, 0)
    const = lambda i: (0, 0)
    lay = lambda i: (layer, 0, 0)
    lay_row = lambda i: (layer * steps + i, 0)
    tab = lambda i: (i % n_tab, 0)
    f32o = jax.ShapeDtypeStruct((n, W), F32)
    bfo = lambda w: jax.ShapeDtypeStruct((n, w), BF16)
    head_rows = jax.ShapeDtypeStruct(k_all.shape, F32)
    out_shape = (f32o, f32o, f32o, f32o, bfo(W), bfo(W), head_rows, bfo(W), head_rows, bfo(2 * W),
                 bfo(W), bfo(2 * W), bfo(2 * W))
    head_spec = pl.BlockSpec((tm * HEADS, HEAD_W), lay_row)
    out_specs = tuple(head_spec if sh is head_rows else pl.BlockSpec((tm, sh.shape[1]), row) for sh in out_shape)
    any_spec = pl.BlockSpec(memory_space=pl.ANY)
    return pl.pallas_call(
        functools.partial(_inproj_kernel, layer=layer),
        grid=(steps,),
        in_specs=[pl.BlockSpec((tm, d), row), pl.BlockSpec((None, 1, d), lay), pl.BlockSpec((None, d, d_in), lay),
                  pl.BlockSpec(lb_logits.shape, const),
                  pl.BlockSpec((tm, HEAD_W), tab), pl.BlockSpec((tm, HEAD_W), tab), pl.BlockSpec((tm, HEAD_W), tab),
                  any_spec, any_spec],
        out_specs=out_specs,
        out_shape=out_shape,
        input_output_aliases={7: 6, 8: 8},
        compiler_params=pltpu.CompilerParams(dimension_semantics=("parallel",), vmem_limit_bytes=V7X_VMEM_LIMIT),
        name="inproj",
    )(x2d, norm_g, w_bf, lb_logits, cos_t, sa_t, sb_t, k_all, v_all)


def _hgrn_kernel(q_ref, g_ref, k_ref, v_ref, s0_ref, o_ref, sfin_ref, st_scr, bc_scr, upd_scr,
                 *, chunk, sub, n_chunks, n_seq):
    j = pl.program_id(1)

    @pl.when(j == 0)
    def _():
        for sq in range(n_seq):
            for hd in range(HEADS):
                st_scr[sq * HEADS + hd] = s0_ref[sq, hd].T

    C, SC = chunk, sub
    n_sub = C // SC
    tri = (_iota((C, C), 0) >= _iota((C, C), 1)).astype(F32).astype(BF16)
    row8 = _iota((SUBLANES, HEAD_W), 0)

    units = [((sq * n_chunks + c) * C, slice(hd * HEAD_W, (hd + 1) * HEAD_W),
              (sq * n_chunks + c) * HEADS + hd, sq * HEADS + hd)
             for sq in range(n_seq) for c in range(n_chunks) for hd in range(HEADS)]


    for r0, hs, ci, _ in units:
        g = g_ref[r0:r0 + C, hs]
        g0 = g.astype(BF16)
        r1 = g - g0.astype(F32)
        g1 = r1.astype(BF16)
        g2 = (r1 - g1.astype(F32)).astype(BF16)
        bc_scr[ci] = _dot(tri, g0) + _dot(tri, g1) + _dot(tri, g2)

    for r0, hs, ci, _ in units:
        btot = bc_scr[ci, C - 1:C, :]
        kdec = (k_ref[r0:r0 + C, hs] * jnp.exp2(btot - bc_scr[ci])).astype(BF16)
        upd_scr[ci] = _dot(v_ref[r0:r0 + C, hs].T.astype(BF16), kdec)

    for r0, hs, ci, si in units:
        st = st_scr[si]
        qdec = (q_ref[r0:r0 + C, hs] * jnp.exp2(bc_scr[ci])).astype(BF16)
        o_ref[r0:r0 + C, hs] = _dot_nt(qdec, st.astype(BF16))
        st_scr[si] = st * jnp.exp2(bc_scr[ci, C - 1:C, :]) + upd_scr[ci]

    if n_sub > 1:
        for r0, hs, ci, _ in units:
            q = q_ref[r0:r0 + C, hs]
            k = k_ref[r0:r0 + C, hs]
            bc = bc_scr[ci]
            a_rows = [jnp.zeros((SC, C), F32)]
            for i in range(1, n_sub):
                bref = bc_scr[ci, i * SC - 1:i * SC, :]
                qi = q[i * SC:(i + 1) * SC] * jnp.exp2(bc[i * SC:(i + 1) * SC] - bref)
                kt = (k[:i * SC] * jnp.exp2(bref - bc[:i * SC])).astype(BF16)
                kt = jnp.concatenate([kt, jnp.zeros((C - i * SC, HEAD_W), BF16)], axis=0)
                a_rows.append(_dot_nt(qi.astype(BF16), kt))
            o_ref[r0:r0 + C, hs] += _dot(jnp.concatenate(a_rows, axis=0).astype(BF16),
                                         v_ref[r0:r0 + C, hs].astype(BF16))

    for r0, hs, ci, _ in units:
        for t0 in range(0, C, SUBLANES):
            c0 = (t0 // SC) * SC
            bg = bc_scr[ci, t0:t0 + SUBLANES, :]
            qg = q_ref[r0 + t0:r0 + t0 + SUBLANES, hs]
            acc = jnp.zeros((SUBLANES, HEAD_W), F32)
            for s in range(c0, t0 + SUBLANES):
                d = bg - bc_scr[ci, s:s + 1, :]
                if s >= t0:
                    d = jnp.where(row8 >= s - t0, d, NEG)
                p = qg * jnp.exp2(d) * k_ref[r0 + s:r0 + s + 1, hs]
                acc = acc + jnp.sum(p, axis=-1, keepdims=True) * v_ref[r0 + s:r0 + s + 1, hs]
            o_ref[r0 + t0:r0 + t0 + SUBLANES, hs] += acc

    @pl.when(j == pl.num_programs(1) - 1)
    def _():
        for sq in range(n_seq):
            for hd in range(HEADS):
                sfin_ref[sq, hd] = st_scr[sq * HEADS + hd].T


def _hgrn(hq, log2f, kh, hi, s0, batch, seq, chunk, sub, tt, n_seq):
    n = batch * seq
    steps = seq // tt
    assert n_seq == 1 or steps == 1
    blk = pl.BlockSpec((n_seq * tt, BRANCH_W), lambda b, j: (b * steps + j, 0))
    st_spec = pl.BlockSpec((n_seq, HEADS, HEAD_W, HEAD_W), lambda b, j: (b, 0, 0, 0))
    return pl.pallas_call(
        functools.partial(_hgrn_kernel, chunk=chunk, sub=sub, n_chunks=tt // chunk, n_seq=n_seq),
        grid=(batch // n_seq, steps),
        in_specs=[blk, blk, blk, blk, st_spec],
        out_specs=(blk, st_spec),
        out_shape=(jax.ShapeDtypeStruct((n, BRANCH_W), F32),
                   jax.ShapeDtypeStruct((batch, HEADS, HEAD_W, HEAD_W), F32)),
        scratch_shapes=[pltpu.VMEM((n_seq * HEADS, HEAD_W, HEAD_W), F32),
                        pltpu.VMEM((n_seq * (tt // chunk) * HEADS, chunk, HEAD_W), F32),
                        pltpu.VMEM((n_seq * (tt // chunk) * HEADS, HEAD_W, HEAD_W), F32)],
        compiler_params=pltpu.CompilerParams(dimension_semantics=("parallel", "arbitrary"),
                                             vmem_limit_bytes=V7X_VMEM_LIMIT),
        name="hgrn",
    )(hq, log2f, kh, hi, s0)


def _stack_maps(q):
    lane = _iota(q.shape, 1)
    zero = jnp.zeros_like(q)
    return jnp.concatenate([jnp.where(lane < SUB_W, q, zero), jnp.where(lane >= SUB_W, q, zero)], axis=0)


def _flash_kernel(q_ref, k_ref, v_ref, lq1_ref, lk1_ref, lq2_ref, lk2_ref, o_ref, qs_scr, s_scr, m_scr, acc_scr,
                  *, tq, kt, rb, layer, lam_init):
    qi = pl.program_id(2)
    tk = kt * tq
    qs_scr[...] = _stack_maps(q_ref[...])
    m_scr[...] = jnp.full(m_scr.shape, NEG, F32)
    acc_scr[...] = jnp.zeros(acc_scr.shape, F32)
    n_rb = 2 * tq // rb

    def scores(r0):
        s_scr[...] = _dot_nt(qs_scr[...], k_ref[pl.ds(r0, tk), :])

    def process(r0, width, shift):
        for b in range(n_rb):
            rows = slice(b * rb, (b + 1) * rb)
            tok0 = (b * rb) % tq
            wb = width if shift is None else min(width, tok0 + rb + shift)
            s = s_scr[rows, :wb]
            if shift is not None:
                tok = _iota((rb, wb), 0) + (tok0 + shift)
                s = jnp.where(_iota((rb, wb), 1) <= tok, s, NEG)
            m_prev = m_scr[rows, :]
            m_new = jnp.maximum(m_prev, jnp.max(s, axis=-1, keepdims=True))
            p = jnp.exp2(s - m_new[:, :1]).astype(BF16)
            alpha = jnp.exp2(m_prev - m_new)
            m_scr[rows, :] = m_new
            pv = _dot(p, v_ref[pl.ds(r0, wb), :])
            acc_scr[rows, :HEAD_W] = alpha * acc_scr[rows, :HEAD_W] + pv[:, :HEAD_W]
            acc_scr[rows, HEAD_W:] = alpha * acc_scr[rows, HEAD_W:] + pv[:, HEAD_W:]

    scores(0)
    n_full = qi // kt

    def body(j, carry):
        r0 = pl.multiple_of(j * tk, tk)
        process(r0, tk, None)
        scores(r0 + tk)
        return carry

    lax.fori_loop(0, n_full, body, 0)
    r0 = pl.multiple_of(n_full * tk, tk)

    for rem in range(kt):
        @pl.when(qi % kt == rem)
        def _():
            process(r0, (rem + 1) * tq, rem * tq)

    lam = _lam(lq1_ref, lk1_ref, lq2_ref, lk2_ref, layer, lam_init)
    o = acc_scr[:, :HEAD_W] / acc_scr[:, HEAD_W:]
    o_ref[...] = o[:tq] - lam * o[tq:]


def _flash(qb, kb, vbx, lams, batch, seq, layer, lam_init, tq, kt, rb):
    n = batch * seq
    nq = seq // tq
    lam_specs = [pl.BlockSpec(a.shape, lambda b, h, i: (0, 0)) for a in lams]
    return pl.pallas_call(
        functools.partial(_flash_kernel, tq=tq, kt=kt, rb=rb, layer=layer, lam_init=lam_init),
        grid=(batch, HEADS, nq),
        in_specs=[pl.BlockSpec((tq, HEAD_W), lambda b, h, i: (b * nq + i, h)),
                  pl.BlockSpec((seq, HEAD_W), lambda b, h, i: (b, h)),
                  pl.BlockSpec((seq, 2 * HEAD_W), lambda b, h, i: (b, h))] + lam_specs,
        out_specs=pl.BlockSpec((tq, HEAD_W), lambda b, h, i: (b * nq + i, h)),
        out_shape=jax.ShapeDtypeStruct((n, BRANCH_W), F32),
        scratch_shapes=[pltpu.VMEM((2 * tq, HEAD_W), BF16), pltpu.VMEM((2 * tq, kt * tq), F32),
                        pltpu.VMEM((2 * tq, HEAD_W), F32), pltpu.VMEM((2 * tq, 2 * HEAD_W), F32)],
        compiler_params=pltpu.CompilerParams(dimension_semantics=("parallel", "parallel", "arbitrary"),
                                             vmem_limit_bytes=V7X_VMEM_LIMIT),
        name="flash",
    )(qb, kb, vbx, *lams)


def _softmax_update(blocks, m_scr, l_scr, acc_scr):
    m_prev = m_scr[...]
    m_cur = jnp.max(functools.reduce(jnp.maximum, [s for s, _ in blocks]), axis=-1, keepdims=True)
    m_new = jnp.maximum(m_prev, m_cur)
    alpha = jnp.exp2(m_prev - m_new)
    p_sum, pv = None, None
    for s, vb in blocks:
        p = jnp.exp2(s - m_new[:, :1])
        d = _dot(p.astype(BF16), vb)
        p_sum = p if p_sum is None else p_sum + p
        pv = d if pv is None else pv + d
    l_scr[...] = alpha * l_scr[...] + jnp.sum(p_sum, axis=-1, keepdims=True)
    acc_scr[...] = alpha * acc_scr[...] + pv
    m_scr[...] = m_new


def _decode_kernel(pt_ref, q_ref, kn_ref, vn_ref, lq1_ref, lk1_ref, lq2_ref, lk2_ref, *rest,
                   pages, layer, lam_init):
    del pt_ref
    k_refs, v_refs = rest[:pages], rest[pages:2 * pages]
    o_ref, m_scr, l_scr, acc_scr = rest[2 * pages:]
    g = pl.program_id(1)

    @pl.when(g == 0)
    def _():
        m_scr[...] = jnp.full(m_scr.shape, NEG, F32)
        l_scr[...] = jnp.zeros(l_scr.shape, F32)
        acc_scr[...] = jnp.zeros(acc_scr.shape, F32)

    qs = _stack_maps(q_ref[0])
    r2 = qs.shape[0]
    rows_k = k_refs[0].shape[0]
    same_head = (_iota((r2, rows_k), 0) % HEADS) == (_iota((r2, rows_k), 1) % HEADS)
    bias = jnp.where(same_head, 0.0, NEG)
    _softmax_update([(_dot_nt(qs, k_refs[p][...].astype(BF16)) + bias, v_refs[p][...].astype(BF16))
                     for p in range(pages)], m_scr, l_scr, acc_scr)

    @pl.when(g == pl.num_programs(1) - 1)
    def _():
        kn = kn_ref[0]
        rn = kn.shape[0]
        ri, ci = _iota((r2, rn), 0), _iota((r2, rn), 1)
        ri = jnp.where(ri >= rn, ri - rn, ri)
        s = jnp.where(ci // HEADS <= ri // HEADS, _dot_nt(qs, kn), NEG)
        s = jnp.where((ri % HEADS) == (ci % HEADS), s, NEG)
        _softmax_update([(s, vn_ref[0])], m_scr, l_scr, acc_scr)
        lam = _lam(lq1_ref, lk1_ref, lq2_ref, lk2_ref, layer, lam_init)
        o = acc_scr[...] / l_scr[...]
        o_ref[0] = o[:rn] - lam * o[rn:]


def _decode(q3, kn3, vn3, lams, ck, cv, page_rows, layer, lam_init, pages):
    bs, rn, _ = q3.shape
    n_pages = page_rows.shape[1]
    rows_k = ck.shape[1]
    new_spec = pl.BlockSpec((1, rn, HEAD_W), lambda b, g, pt: (b, 0, 0))
    lam_specs = [pl.BlockSpec(a.shape, lambda b, g, pt: (0, 0)) for a in lams]

    def page_spec(p):
        return pl.BlockSpec((None, rows_k, HEAD_W), lambda b, g, pt: (pt[b, g * pages + p], 0, 0))

    page_specs = [page_spec(p) for p in range(pages)]
    grid_spec = pltpu.PrefetchScalarGridSpec(
        num_scalar_prefetch=1,
        grid=(bs, n_pages // pages),
        in_specs=[new_spec, new_spec, new_spec] + lam_specs + page_specs + page_specs,
        out_specs=pl.BlockSpec((1, rn, HEAD_W), lambda b, g, pt: (b, 0, 0)),
        scratch_shapes=[pltpu.VMEM((2 * rn, HEAD_W), F32), pltpu.VMEM((2 * rn, HEAD_W), F32),
                        pltpu.VMEM((2 * rn, HEAD_W), F32)],
    )
    return pl.pallas_call(
        functools.partial(_decode_kernel, pages=pages, layer=layer, lam_init=lam_init),
        grid_spec=grid_spec,
        out_shape=jax.ShapeDtypeStruct((bs, rn, HEAD_W), F32),
        compiler_params=pltpu.CompilerParams(dimension_semantics=("parallel", "arbitrary"),
                                             vmem_limit_bytes=V7X_VMEM_LIMIT),
        name="decode",
    )(page_rows, q3, kn3, vn3, *lams, *([ck] * pages), *([cv] * pages))


def _outproj_kernel(oh_ref, hz_ref, od_ref, dz_ref, ga_ref, gb_ref, x_ref, hg_ref, sg_ref,
                    wa_ref, wb_ref, wo_ref, fg_ref, y_ref, *, lam_init, final):
    def headnorm(o, g):
        return jnp.concatenate([_rms(o[:, hd * HEAD_W:(hd + 1) * HEAD_W], g) for hd in range(HEADS)], axis=1)

    f32 = lambda ref: ref[...].astype(F32)
    a = headnorm(oh_ref[...], hg_ref[...]) * f32(hz_ref)
    d = headnorm(od_ref[...], sg_ref[...]) * (1.0 - lam_init) * f32(dz_ref)
    merged = (f32(ga_ref) * _dot(a.astype(BF16), wa_ref[...])
              + f32(gb_ref) * _dot(d.astype(BF16), wb_ref[...]))
    y = x_ref[...] + _dot(merged.astype(BF16), wo_ref[...])
    y_ref[...] = _rms(y, fg_ref[...]) if final else y


def _outproj(oh, hz, od, dz, ga, gb, x2d, hg, sg, wa, wb, wo, fg, layer, lam_init, final, tm):
    n, d = x2d.shape
    row = lambda i: (i, 0)
    const = lambda i: (0, 0)
    lay = lambda i: (layer, 0, 0)
    W = BRANCH_W
    return pl.pallas_call(
        functools.partial(_outproj_kernel, lam_init=lam_init, final=final),
        grid=(n // tm,),
        in_specs=[pl.BlockSpec((tm, W), row), pl.BlockSpec((tm, W), row), pl.BlockSpec((tm, W), row),
                  pl.BlockSpec((tm, W), row), pl.BlockSpec((tm, d), row), pl.BlockSpec((tm, d), row),
                  pl.BlockSpec((tm, d), row), pl.BlockSpec((None, 1, HEAD_W), lay), pl.BlockSpec((None, 1, HEAD_W), lay),
                  pl.BlockSpec((None, W, d), lay), pl.BlockSpec((None, W, d), lay), pl.BlockSpec((None, d, d), lay),
                  pl.BlockSpec((1, d), const)],
        out_specs=pl.BlockSpec((tm, d), row),
        out_shape=jax.ShapeDtypeStruct((n, d), F32),
        compiler_params=pltpu.CompilerParams(dimension_semantics=("parallel",), vmem_limit_bytes=V7X_VMEM_LIMIT),
        name="outproj",
    )(oh, hz, od, dz, ga, gb, x2d, hg, sg, wa, wb, wo, fg)


def _rope_tables(pos):
    dim = jnp.arange(HEAD_W) % SUB_W
    rot = dim < ROT_DIM
    inv_freq = jnp.where(rot, ROPE_THETA ** (-(2 * (dim % (ROT_DIM // 2))).astype(F32) / ROT_DIM), 0.0)
    ang = pos.astype(F32)[:, None] * inv_freq[None, :]
    sin = jnp.sin(ang)
    first = (dim < ROT_DIM // 2)[None, :]
    return jnp.cos(ang), jnp.where(first, -sin, 0.0), jnp.where(first, 0.0, sin)


def kernel(x_prompt, x_sample, cache_k, cache_v, state_hgrn, page_table, norm_g, w_in, hgrn_lb_logits, hgrn_norm_g,
           da_lambda_q1, da_lambda_k1, da_lambda_q2, da_lambda_k2, da_subln_g, w_branch_hgrn, w_branch_attn,
           w_out, final_norm_g):
    bp, sp, d = x_prompt.shape
    bs, t, _ = x_sample.shape
    depth = w_in.shape[0]
    page = cache_k.shape[2]
    past = page_table.shape[1] * page
    W = BRANCH_W

    tm_p = min(256, bp * sp)
    tm_o = min(512, bp * sp)
    tm_s = min(256, bs * t)
    chunk_p = min(64, sp)
    sub_p = min(16, chunk_p)
    tt_p = min(256, sp)
    tq = min(1024, sp // 2)
    kt = 1
    rb = tq // 2
    assert sp % (kt * tq) == 0
    t_pad = 16
    pages = min(32, page_table.shape[1])
    seq_s = math.gcd(bs, 8)

    tabs_p = _rope_tables(jnp.arange(sp))
    tabs_s = tuple(jnp.tile(a, (tm_s // t, 1)) for a in _rope_tables(past + jnp.arange(t)))

    pool = cache_k.shape[1]
    ck = cache_k.reshape(depth * pool, page * HEADS, HEAD_W)
    cv = cache_v.reshape(depth * pool, page * HEADS, HEAD_W)
    s0_p = jnp.zeros((bp, HEADS, HEAD_W, HEAD_W), F32)
    lams = (da_lambda_q1, da_lambda_k1, da_lambda_q2, da_lambda_k2)
    fg = final_norm_g.reshape(1, d)
    w_bf = w_in.astype(BF16)
    ng = norm_g.reshape(depth, 1, d)
    hg = hgrn_norm_g.reshape(depth, 1, HEAD_W)
    sg = da_subln_g.reshape(depth, 1, HEAD_W)
    wa = w_branch_hgrn.astype(BF16)
    wb = w_branch_attn.astype(BF16)
    wo = w_out.astype(BF16)

    xp = x_prompt.reshape(bp * sp, d)
    xs = x_sample.reshape(bs * t, d)
    head_rows = lambda n: jnp.zeros((depth * n * HEADS, HEAD_W), F32)
    kp, vp, ks, vs = head_rows(bp * sp), head_rows(bp * sp), head_rows(bs * t), head_rows(bs * t)
    sp_l, ss_l = [], []
    for l in range(depth):
        lam_init = 0.8 - 0.6 * math.exp(-0.3 * l)
        final = l == depth - 1

        (hq, log2f, kh, hi, hz, qb, kp, kb, vp, vb, dz, ga, gb) = _inproj(
            xp, ng, w_bf, hgrn_lb_logits, *tabs_p, kp, vp, l, tm_p)
        oh, s_fin = _hgrn(hq, log2f, kh, hi, s0_p, bp, sp, chunk_p, sub_p, tt_p, 1)
        od = _flash(qb, kb, vb, lams, bp, sp, l, lam_init, tq, kt, rb)
        xp = _outproj(oh, hz, od, dz, ga, gb, xp, hg, sg, wa, wb, wo, fg, l, lam_init, final, tm_o)
        sp_l.append(s_fin)

        (hq, log2f, kh, hi, hz, qb, ks, kb, vs, vb, dz, ga, gb) = _inproj(
            xs, ng, w_bf, hgrn_lb_logits, *tabs_s, ks, vs, l, tm_s)
        padt = lambda a: jnp.pad(a.reshape(bs, t, W), ((0, 0), (0, t_pad - t), (0, 0))).reshape(bs * t_pad, W)
        oh, s_fin = _hgrn(padt(hq), padt(log2f), padt(kh), padt(hi), state_hgrn[l], bs, t_pad, t_pad, t_pad, t_pad,
                          seq_s)
        oh = oh.reshape(bs, t_pad, W)[:, :t].reshape(bs * t, W)
        rows = lambda a: a.reshape(bs, t * HEADS, HEAD_W)
        vn = vb.reshape(bs, t, HEADS, 2 * HEAD_W)[..., :HEAD_W]
        od = _decode(rows(qb), rows(kb), rows(vn), lams, ck, cv, page_table + l * pool, l, lam_init, pages)
        xs = _outproj(oh, hz, od.reshape(bs * t, W), dz, ga, gb, xs, hg, sg, wa, wb, wo, fg, l, lam_init, final, tm_s)
        ss_l.append(s_fin)

    shape_p = (depth, bp, sp, HEADS, HEAD_W)
    shape_s = (depth, bs, t, HEADS, HEAD_W)
    return (xp.reshape(bp, sp, d), xs.reshape(bs, t, d), kp.reshape(shape_p), vp.reshape(shape_p), jnp.stack(sp_l),
            ks.reshape(shape_s), vs.reshape(shape_s), jnp.stack(ss_l))
```

```python
import functools
import math

import jax
import jax.numpy as jnp
from jax import lax
from jax.experimental import pallas as pl
from jax.experimental.pallas import tpu as pltpu

F32 = jnp.float32
BF16 = jnp.bfloat16

EPS = 1e-6
ROPE_THETA = 500000.0
HEADS = 4
HEAD_W = 128
SUB_W = 64
ROT_DIM = SUB_W // 4
BRANCH_W = HEADS * HEAD_W
NEG = -1e30
QK_SCALE_LOG2 = math.log2(math.e) / math.sqrt(SUB_W)

V7X_VMEM_LIMIT = 56 * 1024 * 1024
SUBLANES = 8


def _dot(a, b):
    return jnp.dot(a, b, preferred_element_type=F32)


def _dot_nt(a, b):
    return lax.dot_general(a, b, (((1,), (1,)), ((), ())), preferred_element_type=F32)


def _iota(shape, dim):
    return lax.broadcasted_iota(jnp.int32, shape, dim)


def _rms(x, g):
    return x * lax.rsqrt(jnp.mean(x * x, axis=-1, keepdims=True) + EPS) * g


def _lam(lq1_ref, lk1_ref, lq2_ref, lk2_ref, layer, lam_init):
    r = slice(layer, layer + 1)
    a = jnp.sum(lq1_ref[r, :] * lk1_ref[r, :], axis=-1, keepdims=True)
    b = jnp.sum(lq2_ref[r, :] * lk2_ref[r, :], axis=-1, keepdims=True)
    return jnp.exp(a) - jnp.exp(b) + lam_init


def _inproj_kernel(x_ref, g_ref, w_ref, lbl_ref, cos_ref, sa_ref, sb_ref, k_all_ref, v_all_ref,
                   hq_ref, log2f_ref, kh_ref, hi_ref, hz_ref, q_ref, k_ref, kb_ref, v_ref, vb_ref,
                   dz_ref, ga_ref, gb_ref, *, layer):
    del k_all_ref, v_all_ref
    h = _rms(x_ref[...], g_ref[...]).astype(BF16)
    W = BRANCH_W

    def proj(i, n=W):
        return _dot(h, w_ref[:, i * W:i * W + n])

    lbl = lbl_ref[...]
    e = jnp.exp(lbl - jnp.max(lbl, axis=0, keepdims=True))
    sm = e / jnp.sum(e, axis=0, keepdims=True)
    lb = jnp.zeros((1, W), F32)
    for i in range(1, layer + 1):
        lb = lb + sm[i:i + 1, :]

    hq_ref[...] = proj(0)
    f = lb + (1.0 - lb) * jax.nn.sigmoid(proj(1))
    log2f_ref[...] = jnp.log2(f)
    kh_ref[...] = 1.0 - f
    hi_ref[...] = proj(2)
    hz_ref[...] = jax.nn.silu(proj(3)).astype(BF16)

    cos, sa, sb = cos_ref[...], sa_ref[...], sb_ref[...]

    def rope(x):
        parts = []
        for hd in range(HEADS):
            xs = x[:, hd * HEAD_W:(hd + 1) * HEAD_W]
            up = pltpu.roll(xs, HEAD_W - ROT_DIM // 2, 1)
            dn = pltpu.roll(xs, ROT_DIM // 2, 1)
            parts.append(xs * cos + up * sa + dn * sb)
        return jnp.concatenate(parts, axis=1)

    def store_head_rows(ref, x):
        for hd in range(HEADS):
            ref[pl.ds(hd, x.shape[0], stride=HEADS), :] = x[:, hd * HEAD_W:(hd + 1) * HEAD_W]

    q_ref[...] = (rope(proj(4)) * QK_SCALE_LOG2).astype(BF16)
    k = rope(proj(5))
    store_head_rows(k_ref, k)
    kb_ref[...] = k.astype(BF16)
    v = proj(6)
    store_head_rows(v_ref, v)
    vb = v.astype(BF16)
    ones = jnp.ones((vb.shape[0], HEAD_W), BF16)
    vb_ref[...] = jnp.concatenate(
        [piece for hd in range(HEADS) for piece in (vb[:, hd * HEAD_W:(hd + 1) * HEAD_W], ones)], axis=1)
    dz_ref[...] = jax.nn.silu(proj(7)).astype(BF16)
    ga_ref[...] = jax.nn.sigmoid(proj(8, 2 * W)).astype(BF16)
    gb_ref[...] = jax.nn.sigmoid(proj(10, 2 * W)).astype(BF16)


def _inproj(x2d, norm_g, w_bf, lb_logits, cos_t, sa_t, sb_t, k_all, v_all, layer, tm):
    n, d = x2d.shape
    d_in = w_bf.shape[2]
    steps = n // tm
    n_tab = cos_t.shape[0] // tm
    W = BRANCH_W
    row = lambda i: (i, 0)
    const = lambda i: (0, 0)
    lay = lambda i: (layer, 0, 0)
    lay_row = lambda i: (layer * steps + i, 0)
    tab = lambda i: (i % n_tab, 0)
    f32o = jax.ShapeDtypeStruct((n, W), F32)
    bfo = lambda w: jax.ShapeDtypeStruct((n, w), BF16)
    head_rows = jax.ShapeDtypeStruct(k_all.shape, F32)
    out_shape = (f32o, f32o, f32o, f32o, bfo(W), bfo(W), head_rows, bfo(W), head_rows, bfo(2 * W),
                 bfo(W), bfo(2 * W), bfo(2 * W))
    head_spec = pl.BlockSpec((tm * HEADS, HEAD_W), lay_row)
    out_specs = tuple(head_spec if sh is head_rows else pl.BlockSpec((tm, sh.shape[1]), row) for sh in out_shape)
    any_spec = pl.BlockSpec(memory_space=pl.ANY)
    return pl.pallas_call(
        functools.partial(_inproj_kernel, layer=layer),
        grid=(steps,),
        in_specs=[pl.BlockSpec((tm, d), row), pl.BlockSpec((None, 1, d), lay), pl.BlockSpec((None, d, d_in), lay),
                  pl.BlockSpec(lb_logits.shape, const),
                  pl.BlockSpec((tm, HEAD_W), tab), pl.BlockSpec((tm, HEAD_W), tab), pl.BlockSpec((tm, HEAD_W), tab),
                  any_spec, any_spec],
        out_specs=out_specs,
        out_shape=out_shape,
        input_output_aliases={7: 6, 8: 8},
        compiler_params=pltpu.CompilerParams(dimension_semantics=("parallel",), vmem_limit_bytes=V7X_VMEM_LIMIT),
        name="inproj",
    )(x2d, norm_g, w_bf, lb_logits, cos_t, sa_t, sb_t, k_all, v_all)


def _hgrn_kernel(q_ref, g_ref, k_ref, v_ref, s0_ref, o_ref, sfin_ref, st_scr, bc_scr, upd_scr,
                 *, chunk, sub, n_chunks, n_seq):
    j = pl.program_id(1)

    @pl.when(j == 0)
    def _():
        for sq in range(n_seq):
            for hd in range(HEADS):
                st_scr[sq * HEADS + hd] = s0_ref[sq, hd].T

    C, SC = chunk, sub
    n_sub = C // SC
    tri = (_iota((C, C), 0) >= _iota((C, C), 1)).astype(F32).astype(BF16)
    row8 = _iota((SUBLANES, HEAD_W), 0)

    units = [((sq * n_chunks + c) * C, slice(hd * HEAD_W, (hd + 1) * HEAD_W),
              (sq * n_chunks + c) * HEADS + hd, sq * HEADS + hd)
             for sq in range(n_seq) for c in range(n_chunks) for hd in range(HEADS)]


    for r0, hs, ci, _ in units:
        g = g_ref[r0:r0 + C, hs]
        g0 = g.astype(BF16)
        r1 = g - g0.astype(F32)
        g1 = r1.astype(BF16)
        g2 = (r1 - g1.astype(F32)).astype(BF16)
        bc_scr[ci] = _dot(tri, g0) + _dot(tri, g1) + _dot(tri, g2)

    for r0, hs, ci, _ in units:
        btot = bc_scr[ci, C - 1:C, :]
        kdec = (k_ref[r0:r0 + C, hs] * jnp.exp2(btot - bc_scr[ci])).astype(BF16)
        upd_scr[ci] = _dot(v_ref[r0:r0 + C, hs].T.astype(BF16), kdec)

    for r0, hs, ci, si in units:
        st = st_scr[si]
        qdec = (q_ref[r0:r0 + C, hs] * jnp.exp2(bc_scr[ci])).astype(BF16)
        o_ref[r0:r0 + C, hs] = _dot_nt(qdec, st.astype(BF16))
        st_scr[si] = st * jnp.exp2(bc_scr[ci, C - 1:C, :]) + upd_scr[ci]

    if n_sub > 1:
        for r0, hs, ci, _ in units:
            q = q_ref[r0:r0 + C, hs]
            k = k_ref[r0:r0 + C, hs]
            bc = bc_scr[ci]
            a_rows = [jnp.zeros((SC, C), F32)]
            for i in range(1, n_sub):
                bref = bc_scr[ci, i * SC - 1:i * SC, :]
                qi = q[i * SC:(i + 1) * SC] * jnp.exp2(bc[i * SC:(i + 1) * SC] - bref)
                kt = (k[:i * SC] * jnp.exp2(bref - bc[:i * SC])).astype(BF16)
                kt = jnp.concatenate([kt, jnp.zeros((C - i * SC, HEAD_W), BF16)], axis=0)
                a_rows.append(_dot_nt(qi.astype(BF16), kt))
            o_ref[r0:r0 + C, hs] += _dot(jnp.concatenate(a_rows, axis=0).astype(BF16),
                                         v_ref[r0:r0 + C, hs].astype(BF16))

    for r0, hs, ci, _ in units:
        for t0 in range(0, C, SUBLANES):
            c0 = (t0 // SC) * SC
            bg = bc_scr[ci, t0:t0 + SUBLANES, :]
            qg = q_ref[r0 + t0:r0 + t0 + SUBLANES, hs]
            acc = jnp.zeros((SUBLANES, HEAD_W), F32)
            for s in range(c0, t0 + SUBLANES):
                d = bg - bc_scr[ci, s:s + 1, :]
                if s >= t0:
                    d = jnp.where(row8 >= s - t0, d, NEG)
                p = qg * jnp.exp2(d) * k_ref[r0 + s:r0 + s + 1, hs]
                acc = acc + jnp.sum(p, axis=-1, keepdims=True) * v_ref[r0 + s:r0 + s + 1, hs]
            o_ref[r0 + t0:r0 + t0 + SUBLANES, hs] += acc

    @pl.when(j == pl.num_programs(1) - 1)
    def _():
        for sq in range(n_seq):
            for hd in range(HEADS):
                sfin_ref[sq, hd] = st_scr[sq * HEADS + hd].T


def _hgrn(hq, log2f, kh, hi, s0, batch, seq, chunk, sub, tt, n_seq):
    n = batch * seq
    steps = seq // tt
    assert n_seq == 1 or steps == 1
    blk = pl.BlockSpec((n_seq * tt, BRANCH_W), lambda b, j: (b * steps + j, 0))
    st_spec = pl.BlockSpec((n_seq, HEADS, HEAD_W, HEAD_W), lambda b, j: (b, 0, 0, 0))
    return pl.pallas_call(
        functools.partial(_hgrn_kernel, chunk=chunk, sub=sub, n_chunks=tt // chunk, n_seq=n_seq),
        grid=(batch // n_seq, steps),
        in_specs=[blk, blk, blk, blk, st_spec],
        out_specs=(blk, st_spec),
        out_shape=(jax.ShapeDtypeStruct((n, BRANCH_W), F32),
                   jax.ShapeDtypeStruct((batch, HEADS, HEAD_W, HEAD_W), F32)),
        scratch_shapes=[pltpu.VMEM((n_seq * HEADS, HEAD_W, HEAD_W), F32),
                        pltpu.VMEM((n_seq * (tt // chunk) * HEADS, chunk, HEAD_W), F32),
                        pltpu.VMEM((n_seq * (tt // chunk) * HEADS, HEAD_W, HEAD_W), F32)],
        compiler_params=pltpu.CompilerParams(dimension_semantics=("parallel", "arbitrary"),
                                             vmem_limit_bytes=V7X_VMEM_LIMIT),
        name="hgrn",
    )(hq, log2f, kh, hi, s0)


def _stack_maps(q):
    lane = _iota(q.shape, 1)
    zero = jnp.zeros_like(q)
    return jnp.concatenate([jnp.where(lane < SUB_W, q, zero), jnp.where(lane >= SUB_W, q, zero)], axis=0)


def _flash_kernel(q_ref, k_ref, v_ref, lq1_ref, lk1_ref, lq2_ref, lk2_ref, o_ref, qs_scr, s_scr, m_scr, acc_scr,
                  *, tq, kt, rb, layer, lam_init):
    qi = pl.program_id(2)
    tk = kt * tq
    qs_scr[...] = _stack_maps(q_ref[...])
    m_scr[...] = jnp.full(m_scr.shape, NEG, F32)
    acc_scr[...] = jnp.zeros(acc_scr.shape, F32)
    n_rb = 2 * tq // rb

    def scores(r0):
        s_scr[...] = _dot_nt(qs_scr[...], k_ref[pl.ds(r0, tk), :])

    def process(r0, width, shift):
        for b in range(n_rb):
            rows = slice(b * rb, (b + 1) * rb)
            tok0 = (b * rb) % tq
            wb = width if shift is None else min(width, tok0 + rb + shift)
            s = s_scr[rows, :wb]
            if shift is not None:
                tok = _iota((rb, wb), 0) + (tok0 + shift)
                s = jnp.where(_iota((rb, wb), 1) <= tok, s, NEG)
            m_prev = m_scr[rows, :]
            m_new = jnp.maximum(m_prev, jnp.max(s, axis=-1, keepdims=True))
            p = jnp.exp2(s - m_new[:, :1]).astype(BF16)
            alpha = jnp.exp2(m_prev - m_new)
            m_scr[rows, :] = m_new
            pv = _dot(p, v_ref[pl.ds(r0, wb), :])
            acc_scr[rows, :HEAD_W] = alpha * acc_scr[rows, :HEAD_W] + pv[:, :HEAD_W]
            acc_scr[rows, HEAD_W:] = alpha * acc_scr[rows, HEAD_W:] + pv[:, HEAD_W:]

    scores(0)
    n_full = qi // kt

    def body(j, carry):
        r0 = pl.multiple_of(j * tk, tk)
        process(r0, tk, None)
        scores(r0 + tk)
        return carry

    lax.fori_loop(0, n_full, body, 0)
    r0 = pl.multiple_of(n_full * tk, tk)

    for rem in range(kt):
        @pl.when(qi % kt == rem)
        def _():
            process(r0, (rem + 1) * tq, rem * tq)

    lam = _lam(lq1_ref, lk1_ref, lq2_ref, lk2_ref, layer, lam_init)
    o = acc_scr[:, :HEAD_W] / acc_scr[:, HEAD_W:]
    o_ref[...] = o[:tq] - lam * o[tq:]


def _flash(qb, kb, vbx, lams, batch, seq, layer, lam_init, tq, kt, rb):
    n = batch * seq
    nq = seq // tq
    lam_specs = [pl.BlockSpec(a.shape, lambda b, h, i: (0, 0)) for a in lams]
    return pl.pallas_call(
        functools.partial(_flash_kernel, tq=tq, kt=kt, rb=rb, layer=layer, lam_init=lam_init),
        grid=(batch, HEADS, nq),
        in_specs=[pl.BlockSpec((tq, HEAD_W), lambda b, h, i: (b * nq + i, h)),
                  pl.BlockSpec((seq, HEAD_W), lambda b, h, i: (b, h)),
                  pl.BlockSpec((seq, 2 * HEAD_W), lambda b, h, i: (b, h))] + lam_specs,
        out_specs=pl.BlockSpec((tq, HEAD_W), lambda b, h, i: (b * nq + i, h)),
        out_shape=jax.ShapeDtypeStruct((n, BRANCH_W), F32),
        scratch_shapes=[pltpu.VMEM((2 * tq, HEAD_W), BF16), pltpu.VMEM((2 * tq, kt * tq), F32),
                        pltpu.VMEM((2 * tq, HEAD_W), F32), pltpu.VMEM((2 * tq, 2 * HEAD_W), F32)],
        compiler_params=pltpu.CompilerParams(dimension_semantics=("parallel", "parallel", "arbitrary"),
                                             vmem_limit_bytes=V7X_VMEM_LIMIT),
        name="flash",
    )(qb, kb, vbx, *lams)


def _softmax_update(blocks, m_scr, l_scr, acc_scr):
    m_prev = m_scr[...]
    m_cur = jnp.max(functools.reduce(jnp.maximum, [s for s, _ in blocks]), axis=-1, keepdims=True)
    m_new = jnp.maximum(m_prev, m_cur)
    alpha = jnp.exp2(m_prev - m_new)
    p_sum, pv = None, None
    for s, vb in blocks:
        p = jnp.exp2(s - m_new[:, :1])
        d = _dot(p.astype(BF16), vb)
        p_sum = p if p_sum is None else p_sum + p
        pv = d if pv is None else pv + d
    l_scr[...] = alpha * l_scr[...] + jnp.sum(p_sum, axis=-1, keepdims=True)
    acc_scr[...] = alpha * acc_scr[...] + pv
    m_scr[...] = m_new


def _decode_kernel(pt_ref, q_ref, kn_ref, vn_ref, lq1_ref, lk1_ref, lq2_ref, lk2_ref, *rest,
                   pages, layer, lam_init):
    del pt_ref
    k_refs, v_refs = rest[:pages], rest[pages:2 * pages]
    o_ref, m_scr, l_scr, acc_scr = rest[2 * pages:]
    g = pl.program_id(1)

    @pl.when(g == 0)
    def _():
        m_scr[...] = jnp.full(m_scr.shape, NEG, F32)
        l_scr[...] = jnp.zeros(l_scr.shape, F32)
        acc_scr[...] = jnp.zeros(acc_scr.shape, F32)

    qs = _stack_maps(q_ref[0])
    r2 = qs.shape[0]
    rows_k = k_refs[0].shape[0]
    same_head = (_iota((r2, rows_k), 0) % HEADS) == (_iota((r2, rows_k), 1) % HEADS)
    bias = jnp.where(same_head, 0.0, NEG)
    _softmax_update([(_dot_nt(qs, k_refs[p][...].astype(BF16)) + bias, v_refs[p][...].astype(BF16))
                     for p in range(pages)], m_scr, l_scr, acc_scr)

    @pl.when(g == pl.num_programs(1) - 1)
    def _():
        kn = kn_ref[0]
        rn = kn.shape[0]
        ri, ci = _iota((r2, rn), 0), _iota((r2, rn), 1)
        ri = jnp.where(ri >= rn, ri - rn, ri)
        s = jnp.where(ci // HEADS <= ri // HEADS, _dot_nt(qs, kn), NEG)
        s = jnp.where((ri % HEADS) == (ci % HEADS), s, NEG)
        _softmax_update([(s, vn_ref[0])], m_scr, l_scr, acc_scr)
        lam = _lam(lq1_ref, lk1_ref, lq2_ref, lk2_ref, layer, lam_init)
        o = acc_scr[...] / l_scr[...]
        o_ref[0] = o[:rn] - lam * o[rn:]


def _decode(q3, kn3, vn3, lams, ck, cv, page_rows, layer, lam_init, pages):
    bs, rn, _ = q3.shape
    n_pages = page_rows.shape[1]
    rows_k = ck.shape[1]
    new_spec = pl.BlockSpec((1, rn, HEAD_W), lambda b, g, pt: (b, 0, 0))
    lam_specs = [pl.BlockSpec(a.shape, lambda b, g, pt: (0, 0)) for a in lams]

    def page_spec(p):
        return pl.BlockSpec((None, rows_k, HEAD_W), lambda b, g, pt: (pt[b, g * pages + p], 0, 0))

    page_specs = [page_spec(p) for p in range(pages)]
    grid_spec = pltpu.PrefetchScalarGridSpec(
        num_scalar_prefetch=1,
        grid=(bs, n_pages // pages),
        in_specs=[new_spec, new_spec, new_spec] + lam_specs + page_specs + page_specs,
        out_specs=pl.BlockSpec((1, rn, HEAD_W), lambda b, g, pt: (b, 0, 0)),
        scratch_shapes=[pltpu.VMEM((2 * rn, HEAD_W), F32), pltpu.VMEM((2 * rn, HEAD_W), F32),
                        pltpu.VMEM((2 * rn, HEAD_W), F32)],
    )
    return pl.pallas_call(
        functools.partial(_decode_kernel, pages=pages, layer=layer, lam_init=lam_init),
        grid_spec=grid_spec,
        out_shape=jax.ShapeDtypeStruct((bs, rn, HEAD_W), F32),
        compiler_params=pltpu.CompilerParams(dimension_semantics=("parallel", "arbitrary"),
                                             vmem_limit_bytes=V7X_VMEM_LIMIT),
        name="decode",
    )(page_rows, q3, kn3, vn3, *lams, *([ck] * pages), *([cv] * pages))


def _outproj_kernel(oh_ref, hz_ref, od_ref, dz_ref, ga_ref, gb_ref, x_ref, hg_ref, sg_ref,
                    wa_ref, wb_ref, wo_ref, fg_ref, y_ref, *, lam_init, final):
    def headnorm(o, g):
        return jnp.concatenate([_rms(o[:, hd * HEAD_W:(hd + 1) * HEAD_W], g) for hd in range(HEADS)], axis=1)

    f32 = lambda ref: ref[...].astype(F32)
    a = headnorm(oh_ref[...], hg_ref[...]) * f32(hz_ref)
    d = headnorm(od_ref[...], sg_ref[...]) * (1.0 - lam_init) * f32(dz_ref)
    merged = (f32(ga_ref) * _dot(a.astype(BF16), wa_ref[...])
              + f32(gb_ref) * _dot(d.astype(BF16), wb_ref[...]))
    y = x_ref[...] + _dot(merged.astype(BF16), wo_ref[...])
    y_ref[...] = _rms(y, fg_ref[...]) if final else y


def _outproj(oh, hz, od, dz, ga, gb, x2d, hg, sg, wa, wb, wo, fg, layer, lam_init, final, tm):
    n, d = x2d.shape
    row = lambda i: (i, 0)
    const = lambda i: (0, 0)
    lay = lambda i: (layer, 0, 0)
    W = BRANCH_W
    return pl.pallas_call(
        functools.partial(_outproj_kernel, lam_init=lam_init, final=final),
        grid=(n // tm,),
        in_specs=[pl.BlockSpec((tm, W), row), pl.BlockSpec((tm, W), row), pl.BlockSpec((tm, W), row),
                  pl.BlockSpec((tm, W), row), pl.BlockSpec((tm, d), row), pl.BlockSpec((tm, d), row),
                  pl.BlockSpec((tm, d), row), pl.BlockSpec((None, 1, HEAD_W), lay), pl.BlockSpec((None, 1, HEAD_W), lay),
                  pl.BlockSpec((None, W, d), lay), pl.BlockSpec((None, W, d), lay), pl.BlockSpec((None, d, d), lay),
                  pl.BlockSpec((1, d), const)],
        out_specs=pl.BlockSpec((tm, d), row),
        out_shape=jax.ShapeDtypeStruct((n, d), F32),
        compiler_params=pltpu.CompilerParams(dimension_semantics=("parallel",), vmem_limit_bytes=V7X_VMEM_LIMIT),
        name="outproj",
    )(oh, hz, od, dz, ga, gb, x2d, hg, sg, wa, wb, wo, fg)


def _rope_tables(pos):
    dim = jnp.arange(HEAD_W) % SUB_W
    rot = dim < ROT_DIM
    inv_freq = jnp.where(rot, ROPE_THETA ** (-(2 * (dim % (ROT_DIM // 2))).astype(F32) / ROT_DIM), 0.0)
    ang = pos.astype(F32)[:, None] * inv_freq[None, :]
    sin = jnp.sin(ang)
    first = (dim < ROT_DIM // 2)[None, :]
    return jnp.cos(ang), jnp.where(first, -sin, 0.0), jnp.where(first, 0.0, sin)


def kernel(x_prompt, x_sample, cache_k, cache_v, state_hgrn, page_table, norm_g, w_in, hgrn_lb_logits, hgrn_norm_g,
           da_lambda_q1, da_lambda_k1, da_lambda_q2, da_lambda_k2, da_subln_g, w_branch_hgrn, w_branch_attn,
           w_out, final_norm_g):
    bp, sp, d = x_prompt.shape
    bs, t, _ = x_sample.shape
    depth = w_in.shape[0]
    page = cache_k.shape[2]
    past = page_table.shape[1] * page
    W = BRANCH_W

    tm_p = min(256, bp * sp)
    tm_o = min(512, bp * sp)
    tm_s = min(256, bs * t)
    chunk_p = min(64, sp)
    sub_p = min(16, chunk_p)
    tt_p = min(512, sp)
    tq = min(1024, sp // 2)
    kt = 1
    rb = tq // 2
    assert sp % (kt * tq) == 0
    t_pad = 16
    pages = min(32, page_table.shape[1])
    seq_s = math.gcd(bs, 8)

    tabs_p = _rope_tables(jnp.arange(sp))
    tabs_s = tuple(jnp.tile(a, (tm_s // t, 1)) for a in _rope_tables(past + jnp.arange(t)))

    pool = cache_k.shape[1]
    ck = cache_k.reshape(depth * pool, page * HEADS, HEAD_W)
    cv = cache_v.reshape(depth * pool, page * HEADS, HEAD_W)
    s0_p = jnp.zeros((bp, HEADS, HEAD_W, HEAD_W), F32)
    lams = (da_lambda_q1, da_lambda_k1, da_lambda_q2, da_lambda_k2)
    fg = final_norm_g.reshape(1, d)
    w_bf = w_in.astype(BF16)
    ng = norm_g.reshape(depth, 1, d)
    hg = hgrn_norm_g.reshape(depth, 1, HEAD_W)
    sg = da_subln_g.reshape(depth, 1, HEAD_W)
    wa = w_branch_hgrn.astype(BF16)
    wb = w_branch_attn.astype(BF16)
    wo = w_out.astype(BF16)

    xp = x_prompt.reshape(bp * sp, d)
    xs = x_sample.reshape(bs * t, d)
    head_rows = lambda n: jnp.zeros((depth * n * HEADS, HEAD_W), F32)
    kp, vp, ks, vs = head_rows(bp * sp), head_rows(bp * sp), head_rows(bs * t), head_rows(bs * t)
    sp_l, ss_l = [], []
    for l in range(depth):
        lam_init = 0.8 - 0.6 * math.exp(-0.3 * l)
        final = l == depth - 1

        (hq, log2f, kh, hi, hz, qb, kp, kb, vp, vb, dz, ga, gb) = _inproj(
            xp, ng, w_bf, hgrn_lb_logits, *tabs_p, kp, vp, l, tm_p)
        oh, s_fin = _hgrn(hq, log2f, kh, hi, s0_p, bp, sp, chunk_p, sub_p, tt_p, 1)
        od = _flash(qb, kb, vb, lams, bp, sp, l, lam_init, tq, kt, rb)
        xp = _outproj(oh, hz, od, dz, ga, gb, xp, hg, sg, wa, wb, wo, fg, l, lam_init, final, tm_o)
        sp_l.append(s_fin)

        (hq, log2f, kh, hi, hz, qb, ks, kb, vs, vb, dz, ga, gb) = _inproj(
            xs, ng, w_bf, hgrn_lb_logits, *tabs_s, ks, vs, l, tm_s)
        padt = lambda a: jnp.pad(a.reshape(bs, t, W), ((0, 0), (0, t_pad - t), (0, 0))).reshape(bs * t_pad, W)
        oh, s_fin = _hgrn(padt(hq), padt(log2f), padt(kh), padt(hi), state_hgrn[l], bs, t_pad, t_pad, t_pad, t_pad,
                          seq_s)
        oh = oh.reshape(bs, t_pad, W)[:, :t].reshape(bs * t, W)
        rows = lambda a: a.reshape(bs, t * HEADS, HEAD_W)
        vn = vb.reshape(bs, t, HEADS, 2 * HEAD_W)[..., :HEAD_W]
        od = _decode(rows(qb), rows(kb), rows(vn), lams, ck, cv, page_table + l * pool, l, lam_init, pages)
        xs = _outproj(oh, hz, od.reshape(bs * t, W), dz, ga, gb, xs, hg, sg, wa, wb, wo, fg, l, lam_init, final, tm_s)
        ss_l.append(s_fin)

    shape_p = (depth, bp, sp, HEADS, HEAD_W)
    shape_s = (depth, bs, t, HEADS, HEAD_W)
    return (xp.reshape(bp, sp, d), xs.reshape(bs, t, d), kp.reshape(shape_p), vp.reshape(shape_p), jnp.stack(sp_l),
            ks.reshape(shape_s), vs.reshape(shape_s), jnp.stack(ss_l))
```
